```python
import math
import jax, jax.numpy as jnp
from jax import lax
import numpy as np

D_MODEL = 1024
BATCH = 16
SEQ = 2048
DEPTH = 1
DEC_BATCH = 8
DEC_SEQ = 8192
PAST_LEN = 128

N_META = 16
ATTN_WIDTH = D_MODEL // 2
POOL_WIDTH = D_MODEL // 2
MIX_WIDTH = ATTN_WIDTH + POOL_WIDTH
N_ATTN_HEADS = 4
ATTN_DV = ATTN_WIDTH // N_ATTN_HEADS
ATTN_DQK = ATTN_DV // 2
QK_COLS = N_ATTN_HEADS * 2 * ATTN_DQK
POOL_WINDOWS = (2, 4, 8, 16)
N_POOL_GROUPS = len(POOL_WINDOWS)
POOL_GROUP = POOL_WIDTH // N_POOL_GROUPS
IN_COLS = 2 * QK_COLS + ATTN_WIDTH + POOL_WIDTH
D_FF = 2816
N_BUCKETS = 32
MAX_DISTANCE = 128
Q_BLOCK = 128
EPS = 1e-6

kernel_name = "hymba_diffattn_pool_macaron_encoder"


def _rmsnorm(x, g):
    xf = x.astype(jnp.float32)
    y = xf * lax.rsqrt(jnp.mean(xf * xf, axis=-1, keepdims=True) + EPS)
    return (y * g.astype(jnp.float32)).astype(x.dtype)


def _swiglu(x, w_gate, w_up, w_down):
    return (jax.nn.silu(x @ w_gate) * (x @ w_up)) @ w_down


def _rel_bucket(rel):
    half = N_BUCKETS // 2
    max_exact = half // 2
    ret = jnp.where(rel > 0, half, 0)
    n = jnp.abs(rel)
    nf = jnp.maximum(n, 1).astype(jnp.float32)
    large = max_exact + (jnp.log(nf / max_exact) / math.log(MAX_DISTANCE / max_exact)
                         * (half - max_exact)).astype(jnp.int32)
    large = jnp.minimum(large, half - 1)
    return ret + jnp.where(n < max_exact, n, large)


def _diff_attention(q, k, v, lam, bias_table):
    B, L = q.shape[0], q.shape[1]
    nb = -(-L // Q_BLOCK)
    Lp = nb * Q_BLOCK
    qp = jnp.pad(q, ((0, 0), (0, Lp - L), (0, 0), (0, 0), (0, 0)))
    qb = qp.reshape(B, nb, Q_BLOCK, N_ATTN_HEADS, 2, ATTN_DQK).transpose(1, 0, 2, 3, 4, 5)
    qpos = jnp.arange(Lp, dtype=jnp.int32).reshape(nb, Q_BLOCK)
    kpos = jnp.arange(L, dtype=jnp.int32)
    scale = ATTN_DQK ** -0.5

    def block(args):
        q_blk, pos = args
        bucket = _rel_bucket(kpos[None, :] - pos[:, None])
        bias = jnp.take(bias_table, bucket, axis=0).astype(jnp.float32)
        bias = bias.transpose(2, 0, 1)
        logits = jnp.einsum('bqhcd,bkhcd->bhcqk', q_blk, k).astype(jnp.float32) * scale
        logits = logits + bias[None, :, None]
        p = jax.nn.softmax(logits, axis=-1)
        a = p[:, :, 0] - lam * p[:, :, 1]
        return jnp.einsum('bhqk,bkhd->bqhd', a.astype(v.dtype), v)

    o = lax.map(block, (qb, qpos))
    o = o.transpose(1, 0, 2, 3, 4).reshape(B, Lp, N_ATTN_HEADS, ATTN_DV)
    return o[:, :L]


def _pool_mixer(u, pool_w, pool_scale):
    B, L = u.shape[0], u.shape[1]
    ug = u.reshape(B, L, N_POOL_GROUPS, POOL_GROUP)
    pos = jnp.arange(L, dtype=jnp.int32)
    outs = []
    for g, w in enumerate(POOL_WINDOWS):
        xg = ug[:, :, g].astype(jnp.float32)
        cs = jnp.concatenate([jnp.zeros((B, 1, POOL_GROUP), jnp.float32),
                              jnp.cumsum(xg, axis=1)], axis=1)
        lo = jnp.clip(pos - w // 2, 0, L)
        hi = jnp.clip(pos + w // 2, 0, L)
        cnt = (hi - lo).astype(jnp.float32)[None, :, None]
        mean = (jnp.take(cs, hi, axis=1) - jnp.take(cs, lo, axis=1)) / cnt
        outs.append((mean - xg).astype(u.dtype))
    pooled = jnp.stack(outs, axis=2)
    mixed = jnp.einsum('blgc,gcd->blgd', pooled, pool_w)
    return mixed.reshape(B, L, POOL_WIDTH) * pool_scale


def _encode(x, meta_tokens, rel_bias_table,
            norm_ffn1, ffn1_w_gate, ffn1_w_up, ffn1_w_down,
            norm_mix, w_in, lambda_q1, lambda_k1, lambda_q2, lambda_k2, subln_gain,
            pool_w, pool_scale, w_out,
            norm_ffn2, ffn2_w_gate, ffn2_w_up, ffn2_w_down, norm_final):
    B = x.shape[0]
    meta = jnp.broadcast_to(meta_tokens[None].astype(x.dtype), (B, N_META, D_MODEL))
    h = jnp.concatenate([meta, x], axis=1)
    L = h.shape[1]
    for l in range(DEPTH):
        lambda_init = 0.8 - 0.6 * math.exp(-0.3 * l)
        h = h + 0.5 * _swiglu(_rmsnorm(h, norm_ffn1[l]), ffn1_w_gate[l], ffn1_w_up[l], ffn1_w_down[l])
        u = _rmsnorm(h, norm_mix[l]) @ w_in[l]
        q = u[..., :QK_COLS].reshape(B, L, N_ATTN_HEADS, 2, ATTN_DQK)
        k = u[..., QK_COLS:2 * QK_COLS].reshape(B, L, N_ATTN_HEADS, 2, ATTN_DQK)
        v = u[..., 2 * QK_COLS:2 * QK_COLS + ATTN_WIDTH].reshape(B, L, N_ATTN_HEADS, ATTN_DV)
        p_in = u[..., 2 * QK_COLS + ATTN_WIDTH:]
        lam = (jnp.exp(jnp.sum(lambda_q1[l].astype(jnp.float32) * lambda_k1[l].astype(jnp.float32)))
               - jnp.exp(jnp.sum(lambda_q2[l].astype(jnp.float32) * lambda_k2[l].astype(jnp.float32)))
               + lambda_init)
        o = _diff_attention(q, k, v, lam, rel_bias_table)
        o = _rmsnorm(o, subln_gain[l]) * (1.0 - lambda_init)
        attn_out = o.reshape(B, L, ATTN_WIDTH)
        pool_out = _pool_mixer(p_in, pool_w[l], pool_scale[l])
        h = h + jnp.concatenate([attn_out, pool_out], axis=-1) @ w_out[l]
        h = h + 0.5 * _swiglu(_rmsnorm(h, norm_ffn2[l]), ffn2_w_gate[l], ffn2_w_up[l], ffn2_w_down[l])
    h = _rmsnorm(h, norm_final)
    return h[:, N_META:]


def setup_inputs(seed: int = 0) -> dict:
    key = jax.random.key(seed)
    ks = jax.random.split(key, 32)
    f32 = jnp.float32

    def nrm(k, shape, scale):
        return jax.random.normal(k, shape, f32) * scale

    def gain(k, shape):
        return 1.0 + 0.02 * jax.random.normal(k, shape, f32)

    return {
        "x_prompt": nrm(ks[0], (BATCH, SEQ, D_MODEL), 1.0),
        "x_sample": nrm(ks[1], (DEC_BATCH, DEC_SEQ, D_MODEL), 1.0),
        "meta_tokens": nrm(ks[2], (N_META, D_MODEL), 1.0),
        "rel_bias_table": nrm(ks[3], (N_BUCKETS, N_ATTN_HEADS), 0.5),
        "norm_ffn1": gain(ks[4], (DEPTH, D_MODEL)),
        "ffn1_w_gate": nrm(ks[5], (DEPTH, D_MODEL, D_FF), D_MODEL ** -0.5),
        "ffn1_w_up": nrm(ks[6], (DEPTH, D_MODEL, D_FF), D_MODEL ** -0.5),
        "ffn1_w_down": nrm(ks[7], (DEPTH, D_FF, D_MODEL), D_FF ** -0.5),
        "norm_mix": gain(ks[8], (DEPTH, D_MODEL)),
        "w_in": nrm(ks[9], (DEPTH, D_MODEL, IN_COLS), D_MODEL ** -0.5),
        "lambda_q1": nrm(ks[10], (DEPTH, ATTN_DQK), 0.1),
        "lambda_k1": nrm(ks[11], (DEPTH, ATTN_DQK), 0.1),
        "lambda_q2": nrm(ks[12], (DEPTH, ATTN_DQK), 0.1),
        "lambda_k2": nrm(ks[13], (DEPTH, ATTN_DQK), 0.1),
        "subln_gain": gain(ks[14], (DEPTH, ATTN_DV)),
        "pool_w": nrm(ks[15], (DEPTH, N_POOL_GROUPS, POOL_GROUP, POOL_GROUP), POOL_GROUP ** -0.5),
        "pool_scale": 1.0 + 0.1 * jax.random.normal(ks[16], (DEPTH, POOL_WIDTH), f32),
        "w_out": nrm(ks[17], (DEPTH, MIX_WIDTH, D_MODEL), MIX_WIDTH ** -0.5),
        "norm_ffn2": gain(ks[18], (DEPTH, D_MODEL)),
        "ffn2_w_gate": nrm(ks[19], (DEPTH, D_MODEL, D_FF), D_MODEL ** -0.5),
        "ffn2_w_up": nrm(ks[20], (DEPTH, D_MODEL, D_FF), D_MODEL ** -0.5),
        "ffn2_w_down": nrm(ks[21], (DEPTH, D_FF, D_MODEL), D_FF ** -0.5),
        "norm_final": gain(ks[22], (D_MODEL,)),
    }


def reference(x_prompt, x_sample, meta_tokens, rel_bias_table,
              norm_ffn1, ffn1_w_gate, ffn1_w_up, ffn1_w_down,
              norm_mix, w_in, lambda_q1, lambda_k1, lambda_q2, lambda_k2, subln_gain,
              pool_w, pool_scale, w_out,
              norm_ffn2, ffn2_w_gate, ffn2_w_up, ffn2_w_down, norm_final):
    y_prompt = _encode(x_prompt, meta_tokens, rel_bias_table,
                       norm_ffn1, ffn1_w_gate, ffn1_w_up, ffn1_w_down,
                       norm_mix, w_in, lambda_q1, lambda_k1, lambda_q2, lambda_k2, subln_gain,
                       pool_w, pool_scale, w_out,
                       norm_ffn2, ffn2_w_gate, ffn2_w_up, ffn2_w_down, norm_final)
    y_sample = _encode(x_sample, meta_tokens, rel_bias_table,
                       norm_ffn1, ffn1_w_gate, ffn1_w_up, ffn1_w_down,
                       norm_mix, w_in, lambda_q1, lambda_k1, lambda_q2, lambda_k2, subln_gain,
                       pool_w, pool_scale, w_out,
                       norm_ffn2, ffn2_w_gate, ffn2_w_up, ffn2_w_down, norm_final)
    return (y_prompt, y_sample)
```

```python
import functools
import math

import numpy as np
import jax
import jax.numpy as jnp
from jax import lax
from jax.experimental import pallas as pl
from jax.experimental.pallas import tpu as pltpu

D_MODEL = 1024
N_META = 16
D_FF = 2816
N_HEADS = 4
HEAD_DV = 128
HEAD_DQK = 64
QK_COLS = N_HEADS * 2 * HEAD_DQK
ATTN_WIDTH = N_HEADS * HEAD_DV
POOL_WINDOWS = (2, 4, 8, 16)
POOL_GROUP = 128
POOL_WIDTH = len(POOL_WINDOWS) * POOL_GROUP
IN_COLS = 2 * QK_COLS + ATTN_WIDTH + POOL_WIDTH
N_BUCKETS = 32
MAX_DISTANCE = 128
EPS = 1e-6
LAMBDA_INIT = 0.8 - 0.6 * math.exp(-0.3 * 0)
QK_SCALE = HEAD_DQK ** -0.5

LANES = 128
F32_SUBLANES = 8
MXU_DIM = 256
VMEM_LIMIT_BYTES = 56 * 1024 * 1024

ROW_TILE = 512
FF_CHUNK = MXU_DIM
N_FF_CHUNKS = D_FF // FF_CHUNK
ATTN_TILE = 256
POOL_HALO = max(POOL_WINDOWS) // 2
MASK_BIAS = -1e30
BIAS_SATURATION_DISTANCE = 91

assert D_FF % FF_CHUNK == 0
assert POOL_HALO == F32_SUBLANES and N_META >= POOL_HALO
assert ATTN_TILE >= BIAS_SATURATION_DISTANCE

F32 = jnp.float32
BF16 = jnp.bfloat16


def _rel_bucket_np(rel):
    half = N_BUCKETS // 2
    max_exact = half // 2
    ret = np.where(rel > 0, half, 0)
    n = np.abs(rel)
    nf = np.maximum(n, 1).astype(np.float64)
    large = max_exact + (np.log(nf / max_exact) / math.log(MAX_DISTANCE / max_exact)
                         * (half - max_exact)).astype(np.int32)
    large = np.minimum(large, half - 1)
    return (ret + np.where(n < max_exact, n, large)).astype(np.int32)


def _check_saturation():
    n = np.arange(BIAS_SATURATION_DISTANCE, 1 << 15)
    assert np.all(_rel_bucket_np(-n) == N_BUCKETS // 2 - 1)
    assert np.all(_rel_bucket_np(n) == N_BUCKETS - 1)


_check_saturation()


def _rms(x, gain):
    return x * lax.rsqrt(jnp.mean(x * x, axis=-1, keepdims=True) + EPS) * gain


def _swiglu_into(acc_ref, xn_ref, wg_ref, wu_ref, wd_ref):
    acc_ref[...] = jnp.zeros_like(acc_ref)

    def body(j, carry):
        xn = xn_ref[...]
        g = jnp.dot(xn, wg_ref[j], preferred_element_type=F32)
        u = jnp.dot(xn, wu_ref[j], preferred_element_type=F32)
        a = (g * (1.0 / (1.0 + jnp.exp(-g))) * u).astype(BF16)
        acc_ref[...] += jnp.dot(a, wd_ref[j], preferred_element_type=F32)
        return carry

    lax.fori_loop(0, N_FF_CHUNKS, body, 0)


def _ffn1_inproj_kernel(x_ref, g1_ref, wg_ref, wu_ref, wd_ref, gm_ref, win_ref, *refs,
                        transposed_heads):
    if transposed_heads:
        h1_ref, qt_ref, k_ref, vt_ref, pool_ref, xn_ref, acc_ref = refs
    else:
        h1_ref, u_ref, xn_ref, acc_ref = refs
    x = x_ref[...]
    xn_ref[...] = _rms(x, g1_ref[...]).astype(BF16)
    _swiglu_into(acc_ref, xn_ref, wg_ref, wu_ref, wd_ref)
    h1 = x + 0.5 * acc_ref[...]
    h1_ref[...] = h1
    hn = _rms(h1, gm_ref[...]).astype(BF16)
    if not transposed_heads:
        u_ref[...] = jnp.dot(hn, win_ref[...], preferred_element_type=F32)
        return
    uq = jnp.dot(hn, win_ref[:, 0:QK_COLS], preferred_element_type=F32) * QK_SCALE
    for h in range(N_HEADS):
        qt_ref[0, h] = uq[:, h * LANES:(h + 1) * LANES].T.astype(BF16)
    uk = jnp.dot(hn, win_ref[:, QK_COLS:2 * QK_COLS], preferred_element_type=F32)
    for h in range(N_HEADS):
        k_ref[0, h] = uk[:, h * LANES:(h + 1) * LANES].astype(BF16)
    uv = jnp.dot(hn, win_ref[:, 2 * QK_COLS:2 * QK_COLS + ATTN_WIDTH],
                 preferred_element_type=F32)
    for h in range(N_HEADS):
        vt_ref[0, h] = uv[:, h * LANES:(h + 1) * LANES].T.astype(BF16)
    pool_ref[...] = jnp.dot(hn, win_ref[:, 2 * QK_COLS + ATTN_WIDTH:],
                            preferred_element_type=F32)


def _resident(shape):
    zeros = (0,) * len(shape)
    return pl.BlockSpec(shape, lambda *_: zeros, pipeline_mode=pl.Buffered(1))


def _ffn1_inproj(x, g1, wg, wu, wd, gm, win, *, seq, name):
    rows = x.shape[0]
    transposed_heads = seq is not None
    tm = ROW_TILE if transposed_heads else rows
    assert rows % tm == 0
    in_specs = [
        pl.BlockSpec((tm, D_MODEL), lambda i: (i, 0)),
        _resident((1, D_MODEL)),
        _resident(wg.shape), _resident(wu.shape), _resident(wd.shape),
        _resident((1, D_MODEL)),
        _resident(win.shape),
    ]
    row_spec = pl.BlockSpec((tm, D_MODEL), lambda i: (i, 0))
    if transposed_heads:
        assert seq % tm == 0
        tpb = seq // tm
        batch = rows // seq
        head_t = pl.BlockSpec((1, N_HEADS, LANES, tm), lambda i: (i // tpb, 0, 0, i % tpb))
        head_n = pl.BlockSpec((1, N_HEADS, tm, LANES), lambda i: (i // tpb, 0, i % tpb, 0))
        out_specs = [row_spec, head_t, head_n, head_t,
                     pl.BlockSpec((tm, POOL_WIDTH), lambda i: (i, 0))]
        out_shape = [
            jax.ShapeDtypeStruct((rows, D_MODEL), F32),
            jax.ShapeDtypeStruct((batch, N_HEADS, LANES, seq), BF16),
            jax.ShapeDtypeStruct((batch, N_HEADS, seq, LANES), BF16),
            jax.ShapeDtypeStruct((batch, N_HEADS, LANES, seq), BF16),
            jax.ShapeDtypeStruct((rows, POOL_WIDTH), F32),
        ]
    else:
        out_specs = [row_spec, pl.BlockSpec((tm, IN_COLS), lambda i: (i, 0))]
        out_shape = [jax.ShapeDtypeStruct((rows, D_MODEL), F32),
                     jax.ShapeDtypeStruct((rows, IN_COLS), F32)]
    return pl.pallas_call(
        functools.partial(_ffn1_inproj_kernel, transposed_heads=transposed_heads),
        grid=(rows // tm,),
        in_specs=in_specs,
        out_specs=out_specs,
        out_shape=out_shape,
        scratch_shapes=[pltpu.VMEM((tm, D_MODEL), BF16), pltpu.VMEM((tm, D_MODEL), F32)],
        compiler_params=pltpu.CompilerParams(
            dimension_semantics=("parallel",), vmem_limit_bytes=VMEM_LIMIT_BYTES),
        name=name,
    )(x, g1, wg, wu, wd, gm, win)


def _bias_tiles_kernel(table_ref, bdiag_ref, bmeta_ref, diag_ref, meta_ref):
    h = pl.program_id(0)
    t = ATTN_TILE

    def lookup(bucket):
        out = jnp.zeros(bucket.shape, F32)
        for b in range(N_BUCKETS):
            out = jnp.where(bucket == b, table_ref[b, h], out)
        return out

    for d in range(3):
        tile = lookup(bdiag_ref[d])
        diag_ref[0, d, :, 0:t] = tile
        diag_ref[0, d, :, t:2 * t] = tile
    bm = bmeta_ref[...]
    near = jnp.where(bm < 0, MASK_BIAS, lookup(bm))
    far = jnp.where(bm < 0, MASK_BIAS, table_ref[N_BUCKETS // 2 - 1, h])
    for c in range(2):
        meta_ref[0, 0, :, c * t:(c + 1) * t] = near
        meta_ref[0, 1, :, c * t:(c + 1) * t] = far


def _bias_tiles(table):
    t = ATTN_TILE
    r = np.arange(t)[:, None]
    c = np.arange(t)[None, :]
    bdiag = np.stack([_rel_bucket_np((d - 1) * t + r - c) for d in range(3)])
    rm = np.arange(LANES)[:, None]
    bmeta = np.where(rm < N_META, _rel_bucket_np(rm - N_META - c), -1).astype(np.int32)
    return pl.pallas_call(
        _bias_tiles_kernel,
        grid=(N_HEADS,),
        in_specs=[
            pl.BlockSpec(memory_space=pltpu.SMEM),
            pl.BlockSpec((3, t, t), lambda h: (0, 0, 0)),
            pl.BlockSpec((LANES, t), lambda h: (0, 0)),
        ],
        out_specs=[
            pl.BlockSpec((1, 3, t, 2 * t), lambda h: (h, 0, 0, 0)),
            pl.BlockSpec((1, 2, LANES, 2 * t), lambda h: (h, 0, 0, 0)),
        ],
        out_shape=[
            jax.ShapeDtypeStruct((N_HEADS, 3, t, 2 * t), F32),
            jax.ShapeDtypeStruct((N_HEADS, 2, LANES, 2 * t), F32),
        ],
        name="bias_tiles",
    )(table, jnp.asarray(bdiag), jnp.asarray(bmeta))


def _attention_kernel(table_ref, qt_ref, k_ref, vt_ref, km_ref, vtm_ref, bdiag_ref, bmeta_ref,
                      lam_ref, gain_ref, out_ref, rhs_ref, m_ref, l_ref, acc_ref, *, n_kv):
    t = ATTN_TILE
    h = pl.program_id(1)
    qi = pl.program_id(2)

    qt = qt_ref[0, 0]
    upper = lax.broadcasted_iota(jnp.int32, qt.shape, 0) < HEAD_DQK
    zero = jnp.zeros_like(qt)
    rhs_ref[:, 0:t] = jnp.where(upper, qt, zero)
    rhs_ref[:, t:2 * t] = jnp.where(upper, zero, qt)

    m_ref[...] = jnp.full(m_ref.shape, MASK_BIAS, F32)
    l_ref[...] = jnp.zeros_like(l_ref)
    acc_ref[...] = jnp.zeros_like(acc_ref)

    def step(kc, vtc, bias):
        s = jnp.dot(kc, rhs_ref[...], preferred_element_type=F32) + bias
        m_old = m_ref[...]
        m_new = jnp.maximum(m_old, jnp.max(s, axis=0, keepdims=True))
        alpha = jnp.exp(m_old - m_new)
        p = jnp.exp(s - m_new)
        l_ref[...] = alpha * l_ref[...] + jnp.sum(p, axis=0, keepdims=True)
        acc_ref[...] = alpha * acc_ref[...] + jnp.dot(
            vtc, p.astype(BF16), preferred_element_type=F32)
        m_ref[...] = m_new

    def main_step(j, bias):
        start = pl.multiple_of(j * t, t)
        step(k_ref[0, 0, pl.ds(start, t), :], vt_ref[0, 0, :, pl.ds(start, t)], bias)

    step(km_ref[0], vtm_ref[0], bmeta_ref[0, 0])

    far_neg = table_ref[N_BUCKETS // 2 - 1, h]
    far_pos = table_ref[N_BUCKETS - 1, h]

    def far_neg_body(j, carry):
        main_step(j, far_neg)
        return carry

    lax.fori_loop(0, jnp.maximum(qi - 1, 0), far_neg_body, 0)

    for d in range(3):
        j = qi - 1 + d

        @pl.when(jnp.logical_and(j >= 0, j < n_kv))
        def _():
            main_step(j, bdiag_ref[0, d])

    def far_pos_body(j, carry):
        main_step(j, far_pos)
        return carry

    lax.fori_loop(qi + 2, n_kv, far_pos_body, 0)

    lp = lam_ref[...]
    lam = (jnp.exp(jnp.sum(lp[0:1] * lp[1:2], axis=-1, keepdims=True))
           - jnp.exp(jnp.sum(lp[2:3] * lp[3:4], axis=-1, keepdims=True)) + LAMBDA_INIT)
    l = l_ref[...]
    o = acc_ref[:, 0:t] / l[:, 0:t] - lam * (acc_ref[:, t:2 * t] / l[:, t:2 * t])
    inv = lax.rsqrt(jnp.mean(o * o, axis=0, keepdims=True) + EPS)
    y = o * inv * gain_ref[...] * (1.0 - LAMBDA_INIT)
    out_ref[0] = y.T.astype(BF16)


def _attention(table, qt, k, vt, k_meta, vt_meta, bdiag, bmeta, lam_params, gain_col, *, name):
    batch, _, _, seq = qt.shape
    t = ATTN_TILE
    assert seq % t == 0
    n_kv = seq // t
    return pl.pallas_call(
        functools.partial(_attention_kernel, n_kv=n_kv),
        grid=(batch, N_HEADS, n_kv),
        in_specs=[
            pl.BlockSpec(memory_space=pltpu.SMEM),
            pl.BlockSpec((1, 1, LANES, t), lambda b, h, q: (b, h, 0, q)),
            pl.BlockSpec((1, 1, seq, LANES), lambda b, h, q: (b, h, 0, 0)),
            pl.BlockSpec((1, 1, LANES, seq), lambda b, h, q: (b, h, 0, 0)),
            pl.BlockSpec((1, LANES, LANES), lambda b, h, q: (h, 0, 0)),
            pl.BlockSpec((1, LANES, LANES), lambda b, h, q: (h, 0, 0)),
            pl.BlockSpec((1, 3, t, 2 * t), lambda b, h, q: (h, 0, 0, 0)),
            pl.BlockSpec((1, 1, LANES, 2 * t), lambda b, h, q: (h, jnp.minimum(q, 1), 0, 0)),
            pl.BlockSpec((4, HEAD_DQK), lambda b, h, q: (0, 0)),
            pl.BlockSpec((HEAD_DV, 1), lambda b, h, q: (0, 0)),
        ],
        out_specs=pl.BlockSpec((1, t, LANES), lambda b, h, q: (b, q, h)),
        out_shape=jax.ShapeDtypeStruct((batch, seq, ATTN_WIDTH), BF16),
        scratch_shapes=[
            pltpu.VMEM((LANES, 2 * t), BF16),
            pltpu.VMEM((1, 2 * t), F32),
            pltpu.VMEM((1, 2 * t), F32),
            pltpu.VMEM((HEAD_DV, 2 * t), F32),
        ],
        compiler_params=pltpu.CompilerParams(
            dimension_semantics=("parallel", "parallel", "arbitrary"),
            vmem_limit_bytes=VMEM_LIMIT_BYTES),
        name=name,
    )(table, qt, k, vt, k_meta, vt_meta, bdiag, bmeta, lam_params, gain_col)


def _mix_ffn2_kernel(h1_ref, attn_ref, pool_ref, prev_ref, next_ref, mhalo_ref, pw_ref, ps_ref,
                     wout_ref, g2_ref, wg_ref, wu_ref, wd_ref, gf_ref, y_ref,
                     x_ref, xn_ref, acc_ref, *, tiles_per_seq, seq):
    tm = h1_ref.shape[0]
    halo = POOL_HALO
    tile = pl.program_id(0) % tiles_per_seq
    x_ref[0:halo, :] = jnp.where(tile == 0, mhalo_ref[...], prev_ref[...])
    x_ref[halo:halo + tm, :] = pool_ref[...]
    x_ref[halo + tm:, :] = jnp.where(tile == tiles_per_seq - 1, 0.0, next_ref[...])

    remaining = seq - (tile * tm + lax.broadcasted_iota(jnp.int32, (tm, 1), 0))
    mix = jnp.dot(attn_ref[...], wout_ref[0:ATTN_WIDTH, :], preferred_element_type=F32)
    for g, w in enumerate(POOL_WINDOWS):
        cols = slice(g * POOL_GROUP, (g + 1) * POOL_GROUP)
        total = x_ref[halo - w // 2:halo - w // 2 + tm, cols]
        for o in range(-w // 2 + 1, w // 2):
            total = total + x_ref[halo + o:halo + o + tm, cols]
        cnt = jnp.minimum(w, remaining + w // 2).astype(F32)
        pooled = (total / cnt - x_ref[halo:halo + tm, cols]).astype(BF16)
        mixed = jnp.dot(pooled, pw_ref[g], preferred_element_type=F32) * ps_ref[:, cols]
        mix = mix + jnp.dot(mixed.astype(BF16),
                            wout_ref[ATTN_WIDTH + g * POOL_GROUP:ATTN_WIDTH + (g + 1) * POOL_GROUP, :],
                            preferred_element_type=F32)
    h2 = h1_ref[...] + mix
    xn_ref[...] = _rms(h2, g2_ref[...]).astype(BF16)
    _swiglu_into(acc_ref, xn_ref, wg_ref, wu_ref, wd_ref)
    y_ref[...] = _rms(h2 + 0.5 * acc_ref[...], gf_ref[...])


def _mix_ffn2(h1, attn, pool, meta_halo, pw, ps, wout, g2, wg, wu, wd, gf, *, seq, name):
    rows = h1.shape[0]
    tm = ROW_TILE
    halo = POOL_HALO
    assert rows % tm == 0 and seq % tm == 0 and tm % halo == 0
    halo_blocks = rows // halo
    per_tile = tm // halo
    return pl.pallas_call(
        functools.partial(_mix_ffn2_kernel, tiles_per_seq=seq // tm, seq=seq),
        grid=(rows // tm,),
        in_specs=[
            pl.BlockSpec((tm, D_MODEL), lambda i: (i, 0)),
            pl.BlockSpec((tm, ATTN_WIDTH), lambda i: (i, 0)),
            pl.BlockSpec((tm, POOL_WIDTH), lambda i: (i, 0)),
            pl.BlockSpec((halo, POOL_WIDTH), lambda i: (jnp.maximum(i * per_tile - 1, 0), 0)),
            pl.BlockSpec((halo, POOL_WIDTH),
                         lambda i: (jnp.minimum((i + 1) * per_tile, halo_blocks - 1), 0)),
            _resident((halo, POOL_WIDTH)),
            _resident(pw.shape),
            _resident((1, POOL_WIDTH)),
            _resident(wout.shape),
            _resident((1, D_MODEL)),
            _resident(wg.shape), _resident(wu.shape), _resident(wd.shape),
            _resident((1, D_MODEL)),
        ],
        out_specs=pl.BlockSpec((tm, D_MODEL), lambda i: (i, 0)),
        out_shape=jax.ShapeDtypeStruct((rows, D_MODEL), F32),
        scratch_shapes=[
            pltpu.VMEM((tm + 2 * halo, POOL_WIDTH), F32),
            pltpu.VMEM((tm, D_MODEL), BF16),
            pltpu.VMEM((tm, D_MODEL), F32),
        ],
        compiler_params=pltpu.CompilerParams(
            dimension_semantics=("parallel",), vmem_limit_bytes=VMEM_LIMIT_BYTES),
        name=name,
    )(h1, attn, pool, pool, pool, meta_halo, pw, ps, wout, g2, wg, wu, wd, gf)


def _chunk_cols(w):
    return w.reshape(D_MODEL, N_FF_CHUNKS, FF_CHUNK).transpose(1, 0, 2).astype(BF16)


def _chunk_rows(w):
    return w.reshape(N_FF_CHUNKS, FF_CHUNK, D_MODEL).astype(BF16)


def kernel(x_prompt, x_sample, meta_tokens, rel_bias_table, norm_ffn1, ffn1_w_gate, ffn1_w_up,
           ffn1_w_down, norm_mix, w_in, lambda_q1, lambda_k1, lambda_q2, lambda_k2, subln_gain,
           pool_w, pool_scale, w_out, norm_ffn2, ffn2_w_gate, ffn2_w_up, ffn2_w_down, norm_final):
    layer = 0
    g1 = norm_ffn1[layer].reshape(1, D_MODEL)
    gm = norm_mix[layer].reshape(1, D_MODEL)
    g2 = norm_ffn2[layer].reshape(1, D_MODEL)
    gf = norm_final.reshape(1, D_MODEL)
    f1 = (_chunk_cols(ffn1_w_gate[layer]), _chunk_cols(ffn1_w_up[layer]),
          _chunk_rows(ffn1_w_down[layer]))
    f2 = (_chunk_cols(ffn2_w_gate[layer]), _chunk_cols(ffn2_w_up[layer]),
          _chunk_rows(ffn2_w_down[layer]))
    win = w_in[layer].astype(BF16)
    wout = w_out[layer].astype(BF16)
    pw = pool_w[layer].astype(BF16)
    ps = pool_scale[layer].reshape(1, POOL_WIDTH)
    lam_params = jnp.stack([lambda_q1[layer], lambda_k1[layer], lambda_q2[layer],
                            lambda_k2[layer]]).astype(F32)
    gain_col = subln_gain[layer].reshape(HEAD_DV, 1)
    table = rel_bias_table.astype(F32)

    _, u_meta = _ffn1_inproj(meta_tokens.astype(F32), g1, *f1, gm, win, seq=None,
                             name="ffn1_inproj_meta")
    pad = LANES - N_META
    k_meta = u_meta[:, QK_COLS:2 * QK_COLS].reshape(N_META, N_HEADS, LANES)
    k_meta = jnp.pad(k_meta.transpose(1, 0, 2), ((0, 0), (0, pad), (0, 0))).astype(BF16)
    v_meta = u_meta[:, 2 * QK_COLS:2 * QK_COLS + ATTN_WIDTH].reshape(N_META, N_HEADS, HEAD_DV)
    vt_meta = jnp.pad(v_meta.transpose(1, 2, 0), ((0, 0), (0, 0), (0, pad))).astype(BF16)
    meta_halo = u_meta[N_META - POOL_HALO:, 2 * QK_COLS + ATTN_WIDTH:]

    bdiag, bmeta = _bias_tiles(table)

    def encode(x, tag):
        batch, seq, _ = x.shape
        rows = batch * seq
        h1, qt, k, vt, pool = _ffn1_inproj(x.reshape(rows, D_MODEL), g1, *f1, gm, win, seq=seq,
                                           name="ffn1_inproj_" + tag)
        attn = _attention(table, qt, k, vt, k_meta, vt_meta, bdiag, bmeta, lam_params, gain_col,
                          name="attention_" + tag)
        y = _mix_ffn2(h1, attn.reshape(rows, ATTN_WIDTH), pool, meta_halo, pw, ps, wout, g2, *f2,
                      gf, seq=seq, name="mix_ffn2_" + tag)
        return y.reshape(batch, seq, D_MODEL)

    return (encode(x_prompt, "prompt"), encode(x_sample, "sample"))
```

```python
import functools
import math

import numpy as np
import jax
import jax.numpy as jnp
from jax import lax
from jax.experimental import pallas as pl
from jax.experimental.pallas import tpu as pltpu

D_MODEL = 1024
N_META = 16
D_FF = 2816
N_HEADS = 4
HEAD_DV = 128
HEAD_DQK = 64
QK_COLS = N_HEADS * 2 * HEAD_DQK
ATTN_WIDTH = N_HEADS * HEAD_DV
POOL_WINDOWS = (2, 4, 8, 16)
POOL_GROUP = 128
POOL_WIDTH = len(POOL_WINDOWS) * POOL_GROUP
IN_COLS = 2 * QK_COLS + ATTN_WIDTH + POOL_WIDTH
N_BUCKETS = 32
MAX_DISTANCE = 128
EPS = 1e-6
LAMBDA_INIT = 0.8 - 0.6 * math.exp(-0.3 * 0)
LOG2E = math.log2(math.e)
QK_SCALE_LOG2 = HEAD_DQK ** -0.5 * LOG2E

LANES = 128
F32_SUBLANES = 8
MXU_DIM = 256
VMEM_LIMIT_BYTES = 56 * 1024 * 1024

ROW_TILE = 512
FF_CHUNK = MXU_DIM
N_FF_CHUNKS = D_FF // FF_CHUNK
ATTN_TILE = 256
POOL_HALO = max(POOL_WINDOWS) // 2
MASK_BIAS = -1e30
BIAS_SATURATION_DISTANCE = 91

assert D_FF % FF_CHUNK == 0
assert POOL_HALO == F32_SUBLANES and N_META >= POOL_HALO
assert ATTN_TILE >= BIAS_SATURATION_DISTANCE

F32 = jnp.float32
BF16 = jnp.bfloat16


def _rel_bucket_np(rel):
    half = N_BUCKETS // 2
    max_exact = half // 2
    ret = np.where(rel > 0, half, 0)
    n = np.abs(rel)
    nf = np.maximum(n, 1).astype(np.float64)
    large = max_exact + (np.log(nf / max_exact) / math.log(MAX_DISTANCE / max_exact)
                         * (half - max_exact)).astype(np.int32)
    large = np.minimum(large, half - 1)
    return (ret + np.where(n < max_exact, n, large)).astype(np.int32)


def _check_saturation():
    n = np.arange(BIAS_SATURATION_DISTANCE, 1 << 15)
    assert np.all(_rel_bucket_np(-n) == N_BUCKETS // 2 - 1)
    assert np.all(_rel_bucket_np(n) == N_BUCKETS - 1)


_check_saturation()


def _rms(x, gain):
    return x * lax.rsqrt(jnp.mean(x * x, axis=-1, keepdims=True) + EPS) * gain


def _swiglu_into(acc_ref, xn_ref, wg_ref, wu_ref, wd_ref):
    acc_ref[...] = jnp.zeros_like(acc_ref)

    def body(j, carry):
        xn = xn_ref[...]
        g = jnp.dot(xn, wg_ref[j], preferred_element_type=F32)
        u = jnp.dot(xn, wu_ref[j], preferred_element_type=F32)
        a = (g * (1.0 / (1.0 + jnp.exp(-g))) * u).astype(BF16)
        acc_ref[...] += jnp.dot(a, wd_ref[j], preferred_element_type=F32)
        return carry

    lax.fori_loop(0, N_FF_CHUNKS, body, 0)


def _ffn1_inproj_kernel(x_ref, g1_ref, wg_ref, wu_ref, wd_ref, gm_ref, win_ref, *refs,
                        transposed_heads):
    if transposed_heads:
        h1_ref, qt_ref, k_ref, vt_ref, pool_ref, xn_ref, acc_ref = refs
    else:
        h1_ref, u_ref, xn_ref, acc_ref = refs
    x = x_ref[...]
    xn_ref[...] = _rms(x, g1_ref[...]).astype(BF16)
    _swiglu_into(acc_ref, xn_ref, wg_ref, wu_ref, wd_ref)
    h1 = x + 0.5 * acc_ref[...]
    h1_ref[...] = h1
    hn = _rms(h1, gm_ref[...]).astype(BF16)
    if not transposed_heads:
        u_ref[...] = jnp.dot(hn, win_ref[...], preferred_element_type=F32)
        return
    uq = jnp.dot(hn, win_ref[:, 0:QK_COLS], preferred_element_type=F32) * QK_SCALE_LOG2
    for h in range(N_HEADS):
        qt_ref[0, h] = uq[:, h * LANES:(h + 1) * LANES].T.astype(BF16)
    uk = jnp.dot(hn, win_ref[:, QK_COLS:2 * QK_COLS], preferred_element_type=F32)
    for h in range(N_HEADS):
        k_ref[0, h] = uk[:, h * LANES:(h + 1) * LANES].astype(BF16)
    uv = jnp.dot(hn, win_ref[:, 2 * QK_COLS:2 * QK_COLS + ATTN_WIDTH],
                 preferred_element_type=F32)
    for h in range(N_HEADS):
        vt_ref[0, h] = uv[:, h * LANES:(h + 1) * LANES].T.astype(BF16)
    pool_ref[...] = jnp.dot(hn, win_ref[:, 2 * QK_COLS + ATTN_WIDTH:],
                            preferred_element_type=F32)


def _resident(shape):
    zeros = (0,) * len(shape)
    return pl.BlockSpec(shape, lambda *_: zeros, pipeline_mode=pl.Buffered(1))


def _ffn1_inproj(x, g1, wg, wu, wd, gm, win, *, seq, name):
    rows = x.shape[0]
    transposed_heads = seq is not None
    tm = ROW_TILE if transposed_heads else rows
    assert rows % tm == 0
    in_specs = [
        pl.BlockSpec((tm, D_MODEL), lambda i: (i, 0)),
        _resident((1, D_MODEL)),
        _resident(wg.shape), _resident(wu.shape), _resident(wd.shape),
        _resident((1, D_MODEL)),
        _resident(win.shape),
    ]
    row_spec = pl.BlockSpec((tm, D_MODEL), lambda i: (i, 0))
    if transposed_heads:
        assert seq % tm == 0
        tpb = seq // tm
        batch = rows // seq
        head_t = pl.BlockSpec((1, N_HEADS, LANES, tm), lambda i: (i // tpb, 0, 0, i % tpb))
        head_n = pl.BlockSpec((1, N_HEADS, tm, LANES), lambda i: (i // tpb, 0, i % tpb, 0))
        out_specs = [row_spec, head_t, head_n, head_t,
                     pl.BlockSpec((tm, POOL_WIDTH), lambda i: (i, 0))]
        out_shape = [
            jax.ShapeDtypeStruct((rows, D_MODEL), F32),
            jax.ShapeDtypeStruct((batch, N_HEADS, LANES, seq), BF16),
            jax.ShapeDtypeStruct((batch, N_HEADS, seq, LANES), BF16),
            jax.ShapeDtypeStruct((batch, N_HEADS, LANES, seq), BF16),
            jax.ShapeDtypeStruct((rows, POOL_WIDTH), F32),
        ]
    else:
        out_specs = [row_spec, pl.BlockSpec((tm, IN_COLS), lambda i: (i, 0))]
        out_shape = [jax.ShapeDtypeStruct((rows, D_MODEL), F32),
                     jax.ShapeDtypeStruct((rows, IN_COLS), F32)]
    return pl.pallas_call(
        functools.partial(_ffn1_inproj_kernel, transposed_heads=transposed_heads),
        grid=(rows // tm,),
        in_specs=in_specs,
        out_specs=out_specs,
        out_shape=out_shape,
        scratch_shapes=[pltpu.VMEM((tm, D_MODEL), BF16), pltpu.VMEM((tm, D_MODEL), F32)],
        compiler_params=pltpu.CompilerParams(
            dimension_semantics=("parallel",), vmem_limit_bytes=VMEM_LIMIT_BYTES),
        name=name,
    )(x, g1, wg, wu, wd, gm, win)


def _bias_tiles_kernel(table_ref, bdiag_ref, bmeta_ref, diag_ref, meta_ref):
    h = pl.program_id(0)
    t = ATTN_TILE

    def lookup(bucket):
        out = jnp.zeros(bucket.shape, F32)
        for b in range(N_BUCKETS):
            out = jnp.where(bucket == b, table_ref[b, h], out)
        return out

    for d in range(3):
        tile = lookup(bdiag_ref[d])
        diag_ref[0, d, :, 0:t] = tile
        diag_ref[0, d, :, t:2 * t] = tile
    bm = bmeta_ref[...]
    near = jnp.where(bm < 0, MASK_BIAS, lookup(bm))
    far = jnp.where(bm < 0, MASK_BIAS, table_ref[N_BUCKETS // 2 - 1, h])
    for c in range(2):
        meta_ref[0, 0, :, c * t:(c + 1) * t] = near
        meta_ref[0, 1, :, c * t:(c + 1) * t] = far


def _bias_tiles(table):
    t = ATTN_TILE
    r = np.arange(t)[:, None]
    c = np.arange(t)[None, :]
    bdiag = np.stack([_rel_bucket_np((d - 1) * t + r - c) for d in range(3)])
    rm = np.arange(LANES)[:, None]
    bmeta = np.where(rm < N_META, _rel_bucket_np(rm - N_META - c), -1).astype(np.int32)
    return pl.pallas_call(
        _bias_tiles_kernel,
        grid=(N_HEADS,),
        in_specs=[
            pl.BlockSpec(memory_space=pltpu.SMEM),
            pl.BlockSpec((3, t, t), lambda h: (0, 0, 0)),
            pl.BlockSpec((LANES, t), lambda h: (0, 0)),
        ],
        out_specs=[
            pl.BlockSpec((1, 3, t, 2 * t), lambda h: (h, 0, 0, 0)),
            pl.BlockSpec((1, 2, LANES, 2 * t), lambda h: (h, 0, 0, 0)),
        ],
        out_shape=[
            jax.ShapeDtypeStruct((N_HEADS, 3, t, 2 * t), F32),
            jax.ShapeDtypeStruct((N_HEADS, 2, LANES, 2 * t), F32),
        ],
        name="bias_tiles",
    )(table, jnp.asarray(bdiag), jnp.asarray(bmeta))


def _attention_kernel(table_ref, qt_ref, k_ref, vt_ref, km_ref, vtm_ref, bdiag_ref, bmeta_ref,
                      lam_ref, gain_ref, out_ref, rhs_ref, m_ref, l_ref, acc_ref, s_ref, p_ref,
                      *, n_kv):
    t = ATTN_TILE
    qi = pl.program_id(1)

    for h in range(N_HEADS):
        qt = qt_ref[0, h]
        upper = lax.broadcasted_iota(jnp.int32, qt.shape, 0) < HEAD_DQK
        zero = jnp.zeros_like(qt)
        rhs_ref[h, :, 0:t] = jnp.where(upper, qt, zero)
        rhs_ref[h, :, t:2 * t] = jnp.where(upper, zero, qt)

    m_ref[...] = jnp.full(m_ref.shape, MASK_BIAS, F32)
    l_ref[...] = jnp.zeros_like(l_ref)
    acc_ref[...] = jnp.zeros_like(acc_ref)

    def logits(h, kc, bias_tile):
        s = jnp.dot(kc, rhs_ref[h], preferred_element_type=F32)
        if bias_tile is not None:
            s = s + bias_tile
        s_ref[h, 0:kc.shape[0]] = s
        return jnp.max(s, axis=0, keepdims=True)

    def weights(h, rows, mc, bias_const):
        if bias_const is not None:
            mc = mc + bias_const
        m_old = m_ref[h]
        m_new = jnp.maximum(m_old, mc)
        alpha = jnp.exp2(m_old - m_new)
        shift = m_new if bias_const is None else m_new - bias_const
        p = jnp.exp2(s_ref[h, 0:rows] - shift)
        p_ref[h, 0:rows] = p.astype(BF16)
        l_ref[h] = alpha * l_ref[h] + jnp.sum(p, axis=0, keepdims=True)
        m_ref[h] = m_new
        return alpha

    def accumulate(h, vtc, alpha):
        acc_ref[h] = alpha * acc_ref[h] + jnp.dot(
            vtc, p_ref[h, 0:vtc.shape[1]], preferred_element_type=F32)

    def block_step(rows, get_k, get_vt, bias_tiles, bias_row):
        mcs = [logits(h, get_k(h), None if bias_tiles is None else bias_tiles(h))
               for h in range(N_HEADS)]
        alphas = [weights(h, rows, mcs[h], None if bias_row is None else table_ref[bias_row, h])
                  for h in range(N_HEADS)]
        for h in range(N_HEADS):
            accumulate(h, get_vt(h), alphas[h])

    def main_step(j, bias_tiles, bias_row):
        start = pl.multiple_of(j * t, t)
        block_step(t, lambda h: k_ref[0, h, pl.ds(start, t), :],
                   lambda h: vt_ref[0, h, :, pl.ds(start, t)], bias_tiles, bias_row)

    block_step(LANES, lambda h: km_ref[h], lambda h: vtm_ref[h], lambda h: bmeta_ref[h, 0], None)

    def far_neg_body(j, carry):
        main_step(j, None, N_BUCKETS // 2 - 1)
        return carry

    lax.fori_loop(0, jnp.maximum(qi - 1, 0), far_neg_body, 0)

    for d in range(3):
        j = qi - 1 + d

        @pl.when(jnp.logical_and(j >= 0, j < n_kv))
        def _():
            main_step(j, lambda h: bdiag_ref[h, d], None)

    def far_pos_body(j, carry):
        main_step(j, None, N_BUCKETS - 1)
        return carry

    lax.fori_loop(qi + 2, n_kv, far_pos_body, 0)

    lp = lam_ref[...]
    lam = (jnp.exp(jnp.sum(lp[0:1] * lp[1:2], axis=-1, keepdims=True))
           - jnp.exp(jnp.sum(lp[2:3] * lp[3:4], axis=-1, keepdims=True)) + LAMBDA_INIT)
    for h in range(N_HEADS):
        l = l_ref[h]
        o = acc_ref[h, :, 0:t] / l[:, 0:t] - lam * (acc_ref[h, :, t:2 * t] / l[:, t:2 * t])
        inv = lax.rsqrt(jnp.mean(o * o, axis=0, keepdims=True) + EPS)
        y = o * inv * gain_ref[...] * (1.0 - LAMBDA_INIT)
        out_ref[0, :, h * LANES:(h + 1) * LANES] = y.T.astype(BF16)


def _attention(table, qt, k, vt, k_meta, vt_meta, bdiag, bmeta, lam_params, gain_col, *, name):
    batch, _, _, seq = qt.shape
    t = ATTN_TILE
    assert seq % t == 0
    n_kv = seq // t
    return pl.pallas_call(
        functools.partial(_attention_kernel, n_kv=n_kv),
        grid=(batch, n_kv),
        in_specs=[
            pl.BlockSpec(memory_space=pltpu.SMEM),
            pl.BlockSpec((1, N_HEADS, LANES, t), lambda b, q: (b, 0, 0, q)),
            pl.BlockSpec((1, N_HEADS, seq, LANES), lambda b, q: (b, 0, 0, 0),
                         pipeline_mode=pl.Buffered(1)),
            pl.BlockSpec((1, N_HEADS, LANES, seq), lambda b, q: (b, 0, 0, 0),
                         pipeline_mode=pl.Buffered(1)),
            _resident(k_meta.shape),
            _resident(vt_meta.shape),
            _resident(bdiag.shape),
            pl.BlockSpec((N_HEADS, 1, LANES, 2 * t), lambda b, q: (0, jnp.minimum(q, 1), 0, 0)),
            _resident((4, HEAD_DQK)),
            _resident((HEAD_DV, 1)),
        ],
        out_specs=pl.BlockSpec((1, t, ATTN_WIDTH), lambda b, q: (b, q, 0)),
        out_shape=jax.ShapeDtypeStruct((batch, seq, ATTN_WIDTH), BF16),
        scratch_shapes=[
            pltpu.VMEM((N_HEADS, LANES, 2 * t), BF16),
            pltpu.VMEM((N_HEADS, 1, 2 * t), F32),
            pltpu.VMEM((N_HEADS, 1, 2 * t), F32),
            pltpu.VMEM((N_HEADS, HEAD_DV, 2 * t), F32),
            pltpu.VMEM((N_HEADS, t, 2 * t), F32),
            pltpu.VMEM((N_HEADS, t, 2 * t), BF16),
        ],
        compiler_params=pltpu.CompilerParams(
            dimension_semantics=("parallel", "arbitrary"),
            vmem_limit_bytes=VMEM_LIMIT_BYTES),
        name=name,
    )(table, qt, k, vt, k_meta, vt_meta, bdiag, bmeta, lam_params, gain_col)


def _mix_ffn2_kernel(h1_ref, attn_ref, pool_ref, prev_ref, next_ref, mhalo_ref, pw_ref, ps_ref,
                     wout_ref, g2_ref, wg_ref, wu_ref, wd_ref, gf_ref, y_ref,
                     x_ref, xn_ref, acc_ref, *, tiles_per_seq, seq):
    tm = h1_ref.shape[0]
    halo = POOL_HALO
    tile = pl.program_id(0) % tiles_per_seq
    x_ref[0:halo, :] = jnp.where(tile == 0, mhalo_ref[...], prev_ref[...])
    x_ref[halo:halo + tm, :] = pool_ref[...]
    x_ref[halo + tm:, :] = jnp.where(tile == tiles_per_seq - 1, 0.0, next_ref[...])

    remaining = seq - (tile * tm + lax.broadcasted_iota(jnp.int32, (tm, 1), 0))
    mix = jnp.dot(attn_ref[...], wout_ref[0:ATTN_WIDTH, :], preferred_element_type=F32)
    for g, w in enumerate(POOL_WINDOWS):
        cols = slice(g * POOL_GROUP, (g + 1) * POOL_GROUP)
        total = x_ref[halo - w // 2:halo - w // 2 + tm, cols]
        for o in range(-w // 2 + 1, w // 2):
            total = total + x_ref[halo + o:halo + o + tm, cols]
        cnt = jnp.minimum(w, remaining + w // 2).astype(F32)
        pooled = (total / cnt - x_ref[halo:halo + tm, cols]).astype(BF16)
        mixed = jnp.dot(pooled, pw_ref[g], preferred_element_type=F32) * ps_ref[:, cols]
        mix = mix + jnp.dot(mixed.astype(BF16),
                            wout_ref[ATTN_WIDTH + g * POOL_GROUP:ATTN_WIDTH + (g + 1) * POOL_GROUP, :],
                            preferred_element_type=F32)
    h2 = h1_ref[...] + mix
    xn_ref[...] = _rms(h2, g2_ref[...]).astype(BF16)
    _swiglu_into(acc_ref, xn_ref, wg_ref, wu_ref, wd_ref)
    y_ref[...] = _rms(h2 + 0.5 * acc_ref[...], gf_ref[...])


def _mix_ffn2(h1, attn, pool, meta_halo, pw, ps, wout, g2, wg, wu, wd, gf, *, seq, name):
    rows = h1.shape[0]
    tm = ROW_TILE
    halo = POOL_HALO
    assert rows % tm == 0 and seq % tm == 0 and tm % halo == 0
    halo_blocks = rows // halo
    per_tile = tm // halo
    return pl.pallas_call(
        functools.partial(_mix_ffn2_kernel, tiles_per_seq=seq // tm, seq=seq),
        grid=(rows // tm,),
        in_specs=[
            pl.BlockSpec((tm, D_MODEL), lambda i: (i, 0)),
            pl.BlockSpec((tm, ATTN_WIDTH), lambda i: (i, 0)),
            pl.BlockSpec((tm, POOL_WIDTH), lambda i: (i, 0)),
            pl.BlockSpec((halo, POOL_WIDTH), lambda i: (jnp.maximum(i * per_tile - 1, 0), 0)),
            pl.BlockSpec((halo, POOL_WIDTH),
                         lambda i: (jnp.minimum((i + 1) * per_tile, halo_blocks - 1), 0)),
            _resident((halo, POOL_WIDTH)),
            _resident(pw.shape),
            _resident((1, POOL_WIDTH)),
            _resident(wout.shape),
            _resident((1, D_MODEL)),
            _resident(wg.shape), _resident(wu.shape), _resident(wd.shape),
            _resident((1, D_MODEL)),
        ],
        out_specs=pl.BlockSpec((tm, D_MODEL), lambda i: (i, 0)),
        out_shape=jax.ShapeDtypeStruct((rows, D_MODEL), F32),
        scratch_shapes=[
            pltpu.VMEM((tm + 2 * halo, POOL_WIDTH), F32),
            pltpu.VMEM((tm, D_MODEL), BF16),
            pltpu.VMEM((tm, D_MODEL), F32),
        ],
        compiler_params=pltpu.CompilerParams(
            dimension_semantics=("parallel",), vmem_limit_bytes=VMEM_LIMIT_BYTES),
        name=name,
    )(h1, attn, pool, pool, pool, meta_halo, pw, ps, wout, g2, wg, wu, wd, gf)


def _chunk_cols(w):
    return w.reshape(D_MODEL, N_FF_CHUNKS, FF_CHUNK).transpose(1, 0, 2).astype(BF16)


def _chunk_rows(w):
    return w.reshape(N_FF_CHUNKS, FF_CHUNK, D_MODEL).astype(BF16)


def kernel(x_prompt, x_sample, meta_tokens, rel_bias_table, norm_ffn1, ffn1_w_gate, ffn1_w_up,
           ffn1_w_down, norm_mix, w_in, lambda_q1, lambda_k1, lambda_q2, lambda_k2, subln_gain,
           pool_w, pool_scale, w_out, norm_ffn2, ffn2_w_gate, ffn2_w_up, ffn2_w_down, norm_final):
    layer = 0
    g1 = norm_ffn1[layer].reshape(1, D_MODEL)
    gm = norm_mix[layer].reshape(1, D_MODEL)
    g2 = norm_ffn2[layer].reshape(1, D_MODEL)
    gf = norm_final.reshape(1, D_MODEL)
    f1 = (_chunk_cols(ffn1_w_gate[layer]), _chunk_cols(ffn1_w_up[layer]),
          _chunk_rows(ffn1_w_down[layer]))
    f2 = (_chunk_cols(ffn2_w_gate[layer]), _chunk_cols(ffn2_w_up[layer]),
          _chunk_rows(ffn2_w_down[layer]))
    win = w_in[layer].astype(BF16)
    wout = w_out[layer].astype(BF16)
    pw = pool_w[layer].astype(BF16)
    ps = pool_scale[layer].reshape(1, POOL_WIDTH)
    lam_params = jnp.stack([lambda_q1[layer], lambda_k1[layer], lambda_q2[layer],
                            lambda_k2[layer]]).astype(F32)
    gain_col = subln_gain[layer].reshape(HEAD_DV, 1)
    table = rel_bias_table.astype(F32) * LOG2E

    _, u_meta = _ffn1_inproj(meta_tokens.astype(F32), g1, *f1, gm, win, seq=None,
                             name="ffn1_inproj_meta")
    pad = LANES - N_META
    k_meta = u_meta[:, QK_COLS:2 * QK_COLS].reshape(N_META, N_HEADS, LANES)
    k_meta = jnp.pad(k_meta.transpose(1, 0, 2), ((0, 0), (0, pad), (0, 0))).astype(BF16)
    v_meta = u_meta[:, 2 * QK_COLS:2 * QK_COLS + ATTN_WIDTH].reshape(N_META, N_HEADS, HEAD_DV)
    vt_meta = jnp.pad(v_meta.transpose(1, 2, 0), ((0, 0), (0, 0), (0, pad))).astype(BF16)
    meta_halo = u_meta[N_META - POOL_HALO:, 2 * QK_COLS + ATTN_WIDTH:]

    bdiag, bmeta = _bias_tiles(table)

    def encode(x, tag):
        batch, seq, _ = x.shape
        rows = batch * seq
        h1, qt, k, vt, pool = _ffn1_inproj(x.reshape(rows, D_MODEL), g1, *f1, gm, win, seq=seq,
                                           name="ffn1_inproj_" + tag)
        attn = _attention(table, qt, k, vt, k_meta, vt_meta, bdiag, bmeta, lam_params, gain_col,
                          name="attention_" + tag)
        y = _mix_ffn2(h1, attn.reshape(rows, ATTN_WIDTH), pool, meta_halo, pw, ps, wout, g2, *f2,
                      gf, seq=seq, name="mix_ffn2_" + tag)
        return y.reshape(batch, seq, D_MODEL)

    return (encode(x_prompt, "prompt"), encode(x_sample, "sample"))
```

```python
import functools
import math

import numpy as np
import jax
import jax.numpy as jnp
from jax import lax
from jax.experimental import pallas as pl
from jax.experimental.pallas import tpu as pltpu

D_MODEL = 1024
N_META = 16
D_FF = 2816
N_HEADS = 4
HEAD_DV = 128
HEAD_DQK = 64
QK_COLS = N_HEADS * 2 * HEAD_DQK
ATTN_WIDTH = N_HEADS * HEAD_DV
POOL_WINDOWS = (2, 4, 8, 16)
POOL_GROUP = 128
POOL_WIDTH = len(POOL_WINDOWS) * POOL_GROUP
IN_COLS = 2 * QK_COLS + ATTN_WIDTH + POOL_WIDTH
N_BUCKETS = 32
MAX_DISTANCE = 128
EPS = 1e-6
LAMBDA_INIT = 0.8 - 0.6 * math.exp(-0.3 * 0)
LOG2E = math.log2(math.e)
QK_SCALE_LOG2 = HEAD_DQK ** -0.5 * LOG2E

LANES = 128
F32_SUBLANES = 8
MXU_DIM = 256
VMEM_LIMIT_BYTES = 56 * 1024 * 1024

ROW_TILE = 512
FF_CHUNK = MXU_DIM
N_FF_CHUNKS = D_FF // FF_CHUNK
ATTN_TILE = 256
POOL_HALO = max(POOL_WINDOWS) // 2
MASK_BIAS = -1e30
BIAS_SATURATION_DISTANCE = 91

BOUNDED_LOGIT_LIMIT = 48.0
BOUNDED_VALUE_LIMIT = 1e15
NORM_MARGIN = 1.05

assert D_FF % FF_CHUNK == 0
assert POOL_HALO == F32_SUBLANES and N_META >= POOL_HALO
assert ATTN_TILE >= BIAS_SATURATION_DISTANCE

F32 = jnp.float32
BF16 = jnp.bfloat16


def _rel_bucket_np(rel):
    half = N_BUCKETS // 2
    max_exact = half // 2
    ret = np.where(rel > 0, half, 0)
    n = np.abs(rel)
    nf = np.maximum(n, 1).astype(np.float64)
    large = max_exact + (np.log(nf / max_exact) / math.log(MAX_DISTANCE / max_exact)
                         * (half - max_exact)).astype(np.int32)
    large = np.minimum(large, half - 1)
    return (ret + np.where(n < max_exact, n, large)).astype(np.int32)


def _check_saturation():
    n = np.arange(BIAS_SATURATION_DISTANCE, 1 << 15)
    assert np.all(_rel_bucket_np(-n) == N_BUCKETS // 2 - 1)
    assert np.all(_rel_bucket_np(n) == N_BUCKETS - 1)


_check_saturation()


def _rms(x, gain):
    return x * lax.rsqrt(jnp.mean(x * x, axis=-1, keepdims=True) + EPS) * gain


def _swiglu_into(acc_ref, xn_ref, wg_ref, wu_ref, wd_ref):
    acc_ref[...] = jnp.zeros_like(acc_ref)

    def body(j, carry):
        xn = xn_ref[...]
        g = jnp.dot(xn, wg_ref[j], preferred_element_type=F32)
        u = jnp.dot(xn, wu_ref[j], preferred_element_type=F32)
        a = (g * (1.0 / (1.0 + jnp.exp(-g))) * u).astype(BF16)
        acc_ref[...] += jnp.dot(a, wd_ref[j], preferred_element_type=F32)
        return carry

    lax.fori_loop(0, N_FF_CHUNKS, body, 0)


def _ffn1_inproj_kernel(x_ref, g1_ref, wg_ref, wu_ref, wd_ref, gm_ref, win_ref, sel_ref, *refs,
                        transposed_heads):
    if transposed_heads:
        h1_ref, qt_ref, k_ref, vt_ref, pool_ref, stat_ref, xn_ref, acc_ref = refs
    else:
        h1_ref, u_ref, xn_ref, acc_ref = refs
    x = x_ref[...]
    xn_ref[...] = _rms(x, g1_ref[...]).astype(BF16)
    _swiglu_into(acc_ref, xn_ref, wg_ref, wu_ref, wd_ref)
    h1 = x + 0.5 * acc_ref[...]
    h1_ref[...] = h1
    hn = _rms(h1, gm_ref[...]).astype(BF16)
    if not transposed_heads:
        u_ref[...] = jnp.dot(hn, win_ref[...], preferred_element_type=F32)
        return
    uq = jnp.dot(hn, win_ref[:, 0:QK_COLS], preferred_element_type=F32) * QK_SCALE_LOG2
    for h in range(N_HEADS):
        qt_ref[0, h] = uq[:, h * LANES:(h + 1) * LANES].T.astype(BF16)
    uk = jnp.dot(hn, win_ref[:, QK_COLS:2 * QK_COLS], preferred_element_type=F32)
    for h in range(N_HEADS):
        k_ref[0, h] = uk[:, h * LANES:(h + 1) * LANES].astype(BF16)
    uv = jnp.dot(hn, win_ref[:, 2 * QK_COLS:2 * QK_COLS + ATTN_WIDTH],
                 preferred_element_type=F32)
    for h in range(N_HEADS):
        vt_ref[0, h] = uv[:, h * LANES:(h + 1) * LANES].T.astype(BF16)
    pool_ref[...] = jnp.dot(hn, win_ref[:, 2 * QK_COLS + ATTN_WIDTH:],
                            preferred_element_type=F32)

    def max_sq_norms(u):
        sq = jnp.dot((u * u).astype(BF16), sel_ref[...], preferred_element_type=F32)
        return jnp.max(sq, axis=0, keepdims=True)

    stat_ref[0, 0:1, :] = max_sq_norms(uq)
    stat_ref[0, 1:2, :] = max_sq_norms(uk)
    stat_ref[0, 2:3, :] = jnp.broadcast_to(jnp.max(jnp.abs(uv), keepdims=True), (1, LANES))
    stat_ref[0, 3:, :] = jnp.zeros((F32_SUBLANES - 3, LANES), F32)


def _resident(shape):
    zeros = (0,) * len(shape)
    return pl.BlockSpec(shape, lambda *_: zeros, pipeline_mode=pl.Buffered(1))


def _ffn1_inproj(x, g1, wg, wu, wd, gm, win, *, seq, name):
    rows = x.shape[0]
    transposed_heads = seq is not None
    tm = ROW_TILE if transposed_heads else rows
    assert rows % tm == 0
    in_specs = [
        pl.BlockSpec((tm, D_MODEL), lambda i: (i, 0)),
        _resident((1, D_MODEL)),
        _resident(wg.shape), _resident(wu.shape), _resident(wd.shape),
        _resident((1, D_MODEL)),
        _resident(win.shape),
        _resident((QK_COLS, LANES)),
    ]
    sel = (np.arange(QK_COLS)[:, None] // HEAD_DQK == np.arange(LANES)[None, :])
    row_spec = pl.BlockSpec((tm, D_MODEL), lambda i: (i, 0))
    if transposed_heads:
        assert seq % tm == 0
        tpb = seq // tm
        batch = rows // seq
        head_t = pl.BlockSpec((1, N_HEADS, LANES, tm), lambda i: (i // tpb, 0, 0, i % tpb))
        head_n = pl.BlockSpec((1, N_HEADS, tm, LANES), lambda i: (i // tpb, 0, i % tpb, 0))
        out_specs = [row_spec, head_t, head_n, head_t,
                     pl.BlockSpec((tm, POOL_WIDTH), lambda i: (i, 0)),
                     pl.BlockSpec((1, F32_SUBLANES, LANES), lambda i: (i, 0, 0))]
        out_shape = [
            jax.ShapeDtypeStruct((rows, D_MODEL), F32),
            jax.ShapeDtypeStruct((batch, N_HEADS, LANES, seq), BF16),
            jax.ShapeDtypeStruct((batch, N_HEADS, seq, LANES), BF16),
            jax.ShapeDtypeStruct((batch, N_HEADS, LANES, seq), BF16),
            jax.ShapeDtypeStruct((rows, POOL_WIDTH), F32),
            jax.ShapeDtypeStruct((rows // tm, F32_SUBLANES, LANES), F32),
        ]
    else:
        out_specs = [row_spec, pl.BlockSpec((tm, IN_COLS), lambda i: (i, 0))]
        out_shape = [jax.ShapeDtypeStruct((rows, D_MODEL), F32),
                     jax.ShapeDtypeStruct((rows, IN_COLS), F32)]
    return pl.pallas_call(
        functools.partial(_ffn1_inproj_kernel, transposed_heads=transposed_heads),
        grid=(rows // tm,),
        in_specs=in_specs,
        out_specs=out_specs,
        out_shape=out_shape,
        scratch_shapes=[pltpu.VMEM((tm, D_MODEL), BF16), pltpu.VMEM((tm, D_MODEL), F32)],
        compiler_params=pltpu.CompilerParams(
            dimension_semantics=("parallel",), vmem_limit_bytes=VMEM_LIMIT_BYTES),
        name=name,
    )(x, g1, wg, wu, wd, gm, win, jnp.asarray(sel, BF16))


def _bias_tiles_kernel(table_ref, bdiag_ref, bmeta_ref, diag_ref, meta_ref):
    h = pl.program_id(0)
    t = ATTN_TILE

    def lookup(bucket):
        out = jnp.zeros(bucket.shape, F32)
        for b in range(N_BUCKETS):
            out = jnp.where(bucket == b, table_ref[b, h], out)
        return out

    for d in range(3):
        tile = lookup(bdiag_ref[d])
        diag_ref[0, d, :, 0:t] = tile
        diag_ref[0, d, :, t:2 * t] = tile
    bm = bmeta_ref[...]
    near = jnp.where(bm < 0, MASK_BIAS, lookup(bm))
    far = jnp.where(bm < 0, MASK_BIAS, table_ref[N_BUCKETS // 2 - 1, h])
    for c in range(2):
        meta_ref[0, 0, :, c * t:(c + 1) * t] = near
        meta_ref[0, 1, :, c * t:(c + 1) * t] = far


def _bias_tiles(table):
    t = ATTN_TILE
    r = np.arange(t)[:, None]
    c = np.arange(t)[None, :]
    bdiag = np.stack([_rel_bucket_np((d - 1) * t + r - c) for d in range(3)])
    rm = np.arange(LANES)[:, None]
    bmeta = np.where(rm < N_META, _rel_bucket_np(rm - N_META - c), -1).astype(np.int32)
    return pl.pallas_call(
        _bias_tiles_kernel,
        grid=(N_HEADS,),
        in_specs=[
            pl.BlockSpec(memory_space=pltpu.SMEM),
            pl.BlockSpec((3, t, t), lambda h: (0, 0, 0)),
            pl.BlockSpec((LANES, t), lambda h: (0, 0)),
        ],
        out_specs=[
            pl.BlockSpec((1, 3, t, 2 * t), lambda h: (h, 0, 0, 0)),
            pl.BlockSpec((1, 2, LANES, 2 * t), lambda h: (h, 0, 0, 0)),
        ],
        out_shape=[
            jax.ShapeDtypeStruct((N_HEADS, 3, t, 2 * t), F32),
            jax.ShapeDtypeStruct((N_HEADS, 2, LANES, 2 * t), F32),
        ],
        name="bias_tiles",
    )(table, jnp.asarray(bdiag), jnp.asarray(bmeta))


def _attention_kernel(flag_ref, table_ref, qt_ref, k_ref, vt_ref, km_ref, vtm_ref, bdiag_ref,
                      bmeta_ref, lam_ref, gain_ref, out_ref, rhs_ref, m_ref, l_ref, acc_ref, s_ref,
                      p_ref, *, n_kv):
    t = ATTN_TILE
    qi = pl.program_id(1)
    far_neg, near, far_pos = 0, 1, 2
    neg_row, pos_row = N_BUCKETS // 2 - 1, N_BUCKETS - 1

    for h in range(N_HEADS):
        qt = qt_ref[0, h]
        upper = lax.broadcasted_iota(jnp.int32, qt.shape, 0) < HEAD_DQK
        zero = jnp.zeros_like(qt)
        rhs_ref[h, :, 0:t] = jnp.where(upper, qt, zero)
        rhs_ref[h, :, t:2 * t] = jnp.where(upper, zero, qt)

    m_ref[...] = jnp.full(m_ref.shape, MASK_BIAS, F32)
    l_ref[...] = jnp.zeros_like(l_ref)
    acc_ref[...] = jnp.zeros_like(acc_ref)

    def logits(h, kc, bias_tile):
        s = jnp.dot(kc, rhs_ref[h], preferred_element_type=F32)
        return s if bias_tile is None else s + bias_tile

    def bounded_step(region, rows, get_k, get_vt, bias_tiles, bias_row):
        del bias_row
        for h in range(N_HEADS):
            p = jnp.exp2(logits(h, get_k(h), None if bias_tiles is None else bias_tiles(h)))
            p_ref[h, 0:rows] = p.astype(BF16)
            l_ref[region, h] += jnp.sum(p, axis=0, keepdims=True)
        for h in range(N_HEADS):
            acc_ref[region, h] += jnp.dot(get_vt(h), p_ref[h, 0:rows],
                                          preferred_element_type=F32)

    def general_step(region, rows, get_k, get_vt, bias_tiles, bias_row):
        del region
        alphas = []
        for h in range(N_HEADS):
            s = logits(h, get_k(h), None if bias_tiles is None else bias_tiles(h))
            s_ref[h, 0:rows] = s
            mc = jnp.max(s, axis=0, keepdims=True)
            bias_const = None if bias_row is None else table_ref[bias_row, h]
            if bias_const is not None:
                mc = mc + bias_const
            m_old = m_ref[h]
            m_new = jnp.maximum(m_old, mc)
            alpha = jnp.exp2(m_old - m_new)
            shift = m_new if bias_const is None else m_new - bias_const
            p = jnp.exp2(s_ref[h, 0:rows] - shift)
            p_ref[h, 0:rows] = p.astype(BF16)
            l_ref[near, h] = alpha * l_ref[near, h] + jnp.sum(p, axis=0, keepdims=True)
            m_ref[h] = m_new
            alphas.append(alpha)
        for h in range(N_HEADS):
            acc_ref[near, h] = alphas[h] * acc_ref[near, h] + jnp.dot(
                get_vt(h), p_ref[h, 0:rows], preferred_element_type=F32)

    def all_blocks(step):
        def main_step(region, j, bias_tiles, bias_row):
            start = pl.multiple_of(j * t, t)
            step(region, t, lambda h: k_ref[0, h, pl.ds(start, t), :],
                 lambda h: vt_ref[0, h, :, pl.ds(start, t)], bias_tiles, bias_row)

        step(near, LANES, lambda h: km_ref[h], lambda h: vtm_ref[h], lambda h: bmeta_ref[h, 0],
             None)

        def far_neg_body(j, carry):
            main_step(far_neg, j, None, neg_row)
            return carry

        lax.fori_loop(0, jnp.maximum(qi - 1, 0), far_neg_body, 0)

        for d in range(3):
            j = qi - 1 + d

            @pl.when(jnp.logical_and(j >= 0, j < n_kv))
            def _():
                main_step(near, j, lambda h: bdiag_ref[h, d], None)

        def far_pos_body(j, carry):
            main_step(far_pos, j, None, pos_row)
            return carry

        lax.fori_loop(qi + 2, n_kv, far_pos_body, 0)

    bounded = flag_ref[pl.program_id(0), qi] != 0

    @pl.when(bounded)
    def _():
        all_blocks(bounded_step)

    @pl.when(jnp.logical_not(bounded))
    def _():
        all_blocks(general_step)

    lp = lam_ref[...]
    lam = (jnp.exp(jnp.sum(lp[0:1] * lp[1:2], axis=-1, keepdims=True))
           - jnp.exp(jnp.sum(lp[2:3] * lp[3:4], axis=-1, keepdims=True)) + LAMBDA_INIT)
    for h in range(N_HEADS):
        wn = jnp.exp2(jnp.full((1, 1), table_ref[neg_row, h], F32))
        wp = jnp.exp2(jnp.full((1, 1), table_ref[pos_row, h], F32))
        l = wn * l_ref[far_neg, h] + l_ref[near, h] + wp * l_ref[far_pos, h]
        acc = wn * acc_ref[far_neg, h] + acc_ref[near, h] + wp * acc_ref[far_pos, h]
        o = acc[:, 0:t] / l[:, 0:t] - lam * (acc[:, t:2 * t] / l[:, t:2 * t])
        inv = lax.rsqrt(jnp.mean(o * o, axis=0, keepdims=True) + EPS)
        y = o * inv * gain_ref[...] * (1.0 - LAMBDA_INIT)
        out_ref[0, :, h * LANES:(h + 1) * LANES] = y.T.astype(BF16)


def _attention(bounded, table, qt, k, vt, k_meta, vt_meta, bdiag, bmeta, lam_params, gain_col, *,
               name):
    batch, _, _, seq = qt.shape
    t = ATTN_TILE
    assert seq % t == 0
    n_kv = seq // t
    return pl.pallas_call(
        functools.partial(_attention_kernel, n_kv=n_kv),
        grid=(batch, n_kv),
        in_specs=[
            pl.BlockSpec(memory_space=pltpu.SMEM),
            pl.BlockSpec(memory_space=pltpu.SMEM),
            pl.BlockSpec((1, N_HEADS, LANES, t), lambda b, q: (b, 0, 0, q)),
            pl.BlockSpec((1, N_HEADS, seq, LANES), lambda b, q: (b, 0, 0, 0),
                         pipeline_mode=pl.Buffered(1)),
            pl.BlockSpec((1, N_HEADS, LANES, seq), lambda b, q: (b, 0, 0, 0),
                         pipeline_mode=pl.Buffered(1)),
            _resident(k_meta.shape),
            _resident(vt_meta.shape),
            _resident(bdiag.shape),
            pl.BlockSpec((N_HEADS, 1, LANES, 2 * t), lambda b, q: (0, jnp.minimum(q, 1), 0, 0)),
            _resident((4, HEAD_DQK)),
            _resident((HEAD_DV, 1)),
        ],
        out_specs=pl.BlockSpec((1, t, ATTN_WIDTH), lambda b, q: (b, q, 0)),
        out_shape=jax.ShapeDtypeStruct((batch, seq, ATTN_WIDTH), BF16),
        scratch_shapes=[
            pltpu.VMEM((N_HEADS, LANES, 2 * t), BF16),
            pltpu.VMEM((N_HEADS, 1, 2 * t), F32),
            pltpu.VMEM((3, N_HEADS, 1, 2 * t), F32),
            pltpu.VMEM((3, N_HEADS, HEAD_DV, 2 * t), F32),
            pltpu.VMEM((N_HEADS, t, 2 * t), F32),
            pltpu.VMEM((N_HEADS, t, 2 * t), BF16),
        ],
        compiler_params=pltpu.CompilerParams(
            dimension_semantics=("parallel", "arbitrary"),
            vmem_limit_bytes=VMEM_LIMIT_BYTES),
        name=name,
    )(bounded, table, qt, k, vt, k_meta, vt_meta, bdiag, bmeta, lam_params, gain_col)


def _bounded_flags(stats, k_meta_sq, v_meta_max, table, *, batch, seq):
    tiles = seq // ROW_TILE
    groups = 2 * N_HEADS
    q_sq = stats[:, 0, :groups].reshape(batch, tiles, groups)
    k_sq = stats[:, 1, :groups].reshape(batch, tiles, groups).max(axis=1)
    k_sq = jnp.maximum(k_sq, k_meta_sq[None, :])
    v_max = jnp.maximum(stats[:, 2, 0].reshape(batch, tiles).max(axis=1), v_meta_max)
    logit_bound = (jnp.sqrt(q_sq * k_sq[:, None, :]).max(axis=-1) * NORM_MARGIN
                   + jnp.max(jnp.abs(table)))
    ok = jnp.logical_and(logit_bound < BOUNDED_LOGIT_LIMIT,
                         (v_max < BOUNDED_VALUE_LIMIT)[:, None])
    return jnp.repeat(ok.astype(jnp.int32), ROW_TILE // ATTN_TILE, axis=1)


def _mix_ffn2_kernel(h1_ref, attn_ref, pool_ref, prev_ref, next_ref, mhalo_ref, pw_ref, ps_ref,
                     wout_ref, g2_ref, wg_ref, wu_ref, wd_ref, gf_ref, y_ref,
                     x_ref, xn_ref, acc_ref, *, tiles_per_seq, seq):
    tm = h1_ref.shape[0]
    halo = POOL_HALO
    tile = pl.program_id(0) % tiles_per_seq
    x_ref[0:halo, :] = jnp.where(tile == 0, mhalo_ref[...], prev_ref[...])
    x_ref[halo:halo + tm, :] = pool_ref[...]
    x_ref[halo + tm:, :] = jnp.where(tile == tiles_per_seq - 1, 0.0, next_ref[...])

    remaining = seq - (tile * tm + lax.broadcasted_iota(jnp.int32, (tm, 1), 0))
    mix = jnp.dot(attn_ref[...], wout_ref[0:ATTN_WIDTH, :], preferred_element_type=F32)
    for g, w in enumerate(POOL_WINDOWS):
        cols = slice(g * POOL_GROUP, (g + 1) * POOL_GROUP)
        total = x_ref[halo - w // 2:halo - w // 2 + tm, cols]
        for o in range(-w // 2 + 1, w // 2):
            total = total + x_ref[halo + o:halo + o + tm, cols]
        cnt = jnp.minimum(w, remaining + w // 2).astype(F32)
        pooled = (total / cnt - x_ref[halo:halo + tm, cols]).astype(BF16)
        mixed = jnp.dot(pooled, pw_ref[g], preferred_element_type=F32) * ps_ref[:, cols]
        mix = mix + jnp.dot(mixed.astype(BF16),
                            wout_ref[ATTN_WIDTH + g * POOL_GROUP:ATTN_WIDTH + (g + 1) * POOL_GROUP, :],
                            preferred_element_type=F32)
    h2 = h1_ref[...] + mix
    xn_ref[...] = _rms(h2, g2_ref[...]).astype(BF16)
    _swiglu_into(acc_ref, xn_ref, wg_ref, wu_ref, wd_ref)
    y_ref[...] = _rms(h2 + 0.5 * acc_ref[...], gf_ref[...])


def _mix_ffn2(h1, attn, pool, meta_halo, pw, ps, wout, g2, wg, wu, wd, gf, *, seq, name):
    rows = h1.shape[0]
    tm = ROW_TILE
    halo = POOL_HALO
    assert rows % tm == 0 and seq % tm == 0 and tm % halo == 0
    halo_blocks = rows // halo
    per_tile = tm // halo
    return pl.pallas_call(
        functools.partial(_mix_ffn2_kernel, tiles_per_seq=seq // tm, seq=seq),
        grid=(rows // tm,),
        in_specs=[
            pl.BlockSpec((tm, D_MODEL), lambda i: (i, 0)),
            pl.BlockSpec((tm, ATTN_WIDTH), lambda i: (i, 0)),
            pl.BlockSpec((tm, POOL_WIDTH), lambda i: (i, 0)),
            pl.BlockSpec((halo, POOL_WIDTH), lambda i: (jnp.maximum(i * per_tile - 1, 0), 0)),
            pl.BlockSpec((halo, POOL_WIDTH),
                         lambda i: (jnp.minimum((i + 1) * per_tile, halo_blocks - 1), 0)),
            _resident((halo, POOL_WIDTH)),
            _resident(pw.shape),
            _resident((1, POOL_WIDTH)),
            _resident(wout.shape),
            _resident((1, D_MODEL)),
            _resident(wg.shape), _resident(wu.shape), _resident(wd.shape),
            _resident((1, D_MODEL)),
        ],
        out_specs=pl.BlockSpec((tm, D_MODEL), lambda i: (i, 0)),
        out_shape=jax.ShapeDtypeStruct((rows, D_MODEL), F32),
        scratch_shapes=[
            pltpu.VMEM((tm + 2 * halo, POOL_WIDTH), F32),
            pltpu.VMEM((tm, D_MODEL), BF16),
            pltpu.VMEM((tm, D_MODEL), F32),
        ],
        compiler_params=pltpu.CompilerParams(
            dimension_semantics=("parallel",), vmem_limit_bytes=VMEM_LIMIT_BYTES),
        name=name,
    )(h1, attn, pool, pool, pool, meta_halo, pw, ps, wout, g2, wg, wu, wd, gf)


def _chunk_cols(w):
    return w.reshape(D_MODEL, N_FF_CHUNKS, FF_CHUNK).transpose(1, 0, 2).astype(BF16)


def _chunk_rows(w):
    return w.reshape(N_FF_CHUNKS, FF_CHUNK, D_MODEL).astype(BF16)


def kernel(x_prompt, x_sample, meta_tokens, rel_bias_table, norm_ffn1, ffn1_w_gate, ffn1_w_up,
           ffn1_w_down, norm_mix, w_in, lambda_q1, lambda_k1, lambda_q2, lambda_k2, subln_gain,
           pool_w, pool_scale, w_out, norm_ffn2, ffn2_w_gate, ffn2_w_up, ffn2_w_down, norm_final):
    layer = 0
    g1 = norm_ffn1[layer].reshape(1, D_MODEL)
    gm = norm_mix[layer].reshape(1, D_MODEL)
    g2 = norm_ffn2[layer].reshape(1, D_MODEL)
    gf = norm_final.reshape(1, D_MODEL)
    f1 = (_chunk_cols(ffn1_w_gate[layer]), _chunk_cols(ffn1_w_up[layer]),
          _chunk_rows(ffn1_w_down[layer]))
    f2 = (_chunk_cols(ffn2_w_gate[layer]), _chunk_cols(ffn2_w_up[layer]),
          _chunk_rows(ffn2_w_down[layer]))
    win = w_in[layer].astype(BF16)
    wout = w_out[layer].astype(BF16)
    pw = pool_w[layer].astype(BF16)
    ps = pool_scale[layer].reshape(1, POOL_WIDTH)
    lam_params = jnp.stack([lambda_q1[layer], lambda_k1[layer], lambda_q2[layer],
                            lambda_k2[layer]]).astype(F32)
    gain_col = subln_gain[layer].reshape(HEAD_DV, 1)
    table = rel_bias_table.astype(F32) * LOG2E

    _, u_meta = _ffn1_inproj(meta_tokens.astype(F32), g1, *f1, gm, win, seq=None,
                             name="ffn1_inproj_meta")
    pad = LANES - N_META
    k_meta = u_meta[:, QK_COLS:2 * QK_COLS].reshape(N_META, N_HEADS, LANES)
    k_meta = jnp.pad(k_meta.transpose(1, 0, 2), ((0, 0), (0, pad), (0, 0))).astype(BF16)
    v_meta = u_meta[:, 2 * QK_COLS:2 * QK_COLS + ATTN_WIDTH].reshape(N_META, N_HEADS, HEAD_DV)
    vt_meta = jnp.pad(v_meta.transpose(1, 2, 0), ((0, 0), (0, 0), (0, pad))).astype(BF16)
    meta_halo = u_meta[N_META - POOL_HALO:, 2 * QK_COLS + ATTN_WIDTH:]
    k_meta_sq = jnp.square(u_meta[:, QK_COLS:2 * QK_COLS]).reshape(
        N_META, 2 * N_HEADS, HEAD_DQK).sum(axis=-1).max(axis=0)
    v_meta_max = jnp.max(jnp.abs(v_meta))

    bdiag, bmeta = _bias_tiles(table)

    def encode(x, tag):
        batch, seq, _ = x.shape
        rows = batch * seq
        h1, qt, k, vt, pool, stats = _ffn1_inproj(x.reshape(rows, D_MODEL), g1, *f1, gm, win,
                                                  seq=seq, name="ffn1_inproj_" + tag)
        bounded = _bounded_flags(stats, k_meta_sq, v_meta_max, table, batch=batch, seq=seq)
        attn = _attention(bounded, table, qt, k, vt, k_meta, vt_meta, bdiag, bmeta, lam_params,
                          gain_col, name="attention_" + tag)
        y = _mix_ffn2(h1, attn.reshape(rows, ATTN_WIDTH), pool, meta_halo, pw, ps, wout, g2, *f2,
                      gf, seq=seq, name="mix_ffn2_" + tag)
        return y.reshape(batch, seq, D_MODEL)

    return (encode(x_prompt, "prompt"), encode(x_sample, "sample"))
```

```python
import functools
import math

import numpy as np
import jax
import jax.numpy as jnp
from jax import lax
from jax.experimental import pallas as pl
from jax.experimental.pallas import tpu as pltpu

D_MODEL = 1024
N_META = 16
D_FF = 2816
N_HEADS = 4
HEAD_DV = 128
HEAD_DQK = 64
QK_COLS = N_HEADS * 2 * HEAD_DQK
ATTN_WIDTH = N_HEADS * HEAD_DV
POOL_WINDOWS = (2, 4, 8, 16)
POOL_GROUP = 128
POOL_WIDTH = len(POOL_WINDOWS) * POOL_GROUP
IN_COLS = 2 * QK_COLS + ATTN_WIDTH + POOL_WIDTH
N_BUCKETS = 32
MAX_DISTANCE = 128
EPS = 1e-6
LAMBDA_INIT = 0.8 - 0.6 * math.exp(-0.3 * 0)
LOG2E = math.log2(math.e)
QK_SCALE_LOG2 = HEAD_DQK ** -0.5 * LOG2E

LANES = 128
F32_SUBLANES = 8
MXU_DIM = 256
VMEM_LIMIT_BYTES = 56 * 1024 * 1024

ROW_TILE = 512
FF_CHUNK = MXU_DIM
N_FF_CHUNKS = D_FF // FF_CHUNK
ATTN_TILE = 256
POOL_HALO = max(POOL_WINDOWS) // 2
MASK_BIAS = -1e30
BIAS_SATURATION_DISTANCE = 91
KEY_PAIR_UNROLL = 2
FF_CHUNK_UNROLL = True
N_BIAS_TILES = 5

BOUNDED_LOGIT_LIMIT = 48.0
BOUNDED_VALUE_LIMIT = 1e15
NORM_MARGIN = 1.05

assert D_FF % FF_CHUNK == 0
assert POOL_HALO == F32_SUBLANES and N_META >= POOL_HALO
assert ATTN_TILE >= BIAS_SATURATION_DISTANCE

F32 = jnp.float32
BF16 = jnp.bfloat16


def _rel_bucket_np(rel):
    half = N_BUCKETS // 2
    max_exact = half // 2
    ret = np.where(rel > 0, half, 0)
    n = np.abs(rel)
    nf = np.maximum(n, 1).astype(np.float64)
    large = max_exact + (np.log(nf / max_exact) / math.log(MAX_DISTANCE / max_exact)
                         * (half - max_exact)).astype(np.int32)
    large = np.minimum(large, half - 1)
    return (ret + np.where(n < max_exact, n, large)).astype(np.int32)


def _check_saturation():
    n = np.arange(BIAS_SATURATION_DISTANCE, 1 << 15)
    assert np.all(_rel_bucket_np(-n) == N_BUCKETS // 2 - 1)
    assert np.all(_rel_bucket_np(n) == N_BUCKETS - 1)


_check_saturation()


def _rms(x, gain):
    return x * lax.rsqrt(jnp.mean(x * x, axis=-1, keepdims=True) + EPS) * gain


def _swiglu_into(acc_ref, xn_ref, wg_ref, wu_ref, wd_ref):
    acc_ref[...] = jnp.zeros_like(acc_ref)

    def body(j, carry):
        xn = xn_ref[...]
        g = jnp.dot(xn, wg_ref[j], preferred_element_type=F32)
        u = jnp.dot(xn, wu_ref[j], preferred_element_type=F32)
        a = (g * (1.0 / (1.0 + jnp.exp(-g))) * u).astype(BF16)
        acc_ref[...] += jnp.dot(a, wd_ref[j], preferred_element_type=F32)
        return carry

    lax.fori_loop(0, N_FF_CHUNKS, body, 0, unroll=FF_CHUNK_UNROLL)


def _ffn1_inproj_kernel(x_ref, g1_ref, wg_ref, wu_ref, wd_ref, gm_ref, win_ref, sel_ref, *refs,
                        transposed_heads):
    if transposed_heads:
        h1_ref, qt_ref, k_ref, vt_ref, pool_ref, stat_ref, xn_ref, acc_ref = refs
    else:
        h1_ref, u_ref, xn_ref, acc_ref = refs
    x = x_ref[...]
    xn_ref[...] = _rms(x, g1_ref[...]).astype(BF16)
    _swiglu_into(acc_ref, xn_ref, wg_ref, wu_ref, wd_ref)
    h1 = x + 0.5 * acc_ref[...]
    h1_ref[...] = h1
    hn = _rms(h1, gm_ref[...]).astype(BF16)
    if not transposed_heads:
        u_ref[...] = jnp.dot(hn, win_ref[...], preferred_element_type=F32)
        return
    uq = jnp.dot(hn, win_ref[:, 0:QK_COLS], preferred_element_type=F32) * QK_SCALE_LOG2
    for h in range(N_HEADS):
        qt_ref[0, h] = uq[:, h * LANES:(h + 1) * LANES].T.astype(BF16)
    uk = jnp.dot(hn, win_ref[:, QK_COLS:2 * QK_COLS], preferred_element_type=F32)
    for h in range(N_HEADS):
        k_ref[0, h] = uk[:, h * LANES:(h + 1) * LANES].astype(BF16)
    uv = jnp.dot(hn, win_ref[:, 2 * QK_COLS:2 * QK_COLS + ATTN_WIDTH],
                 preferred_element_type=F32)
    for h in range(N_HEADS):
        vt_ref[0, h] = uv[:, h * LANES:(h + 1) * LANES].T.astype(BF16)
    pool_ref[...] = jnp.dot(hn, win_ref[:, 2 * QK_COLS + ATTN_WIDTH:],
                            preferred_element_type=F32)

    def max_sq_norms(u):
        sq = jnp.dot((u * u).astype(BF16), sel_ref[...], preferred_element_type=F32)
        return jnp.max(sq, axis=0, keepdims=True)

    stat_ref[0, 0:1, :] = max_sq_norms(uq)
    stat_ref[0, 1:2, :] = max_sq_norms(uk)
    stat_ref[0, 2:3, :] = jnp.broadcast_to(jnp.max(jnp.abs(uv), keepdims=True), (1, LANES))
    stat_ref[0, 3:, :] = jnp.zeros((F32_SUBLANES - 3, LANES), F32)


def _resident(shape):
    zeros = (0,) * len(shape)
    return pl.BlockSpec(shape, lambda *_: zeros, pipeline_mode=pl.Buffered(1))


def _ffn1_inproj(x, g1, wg, wu, wd, gm, win, *, seq, name):
    rows = x.shape[0]
    transposed_heads = seq is not None
    tm = ROW_TILE if transposed_heads else rows
    assert rows % tm == 0
    in_specs = [
        pl.BlockSpec((tm, D_MODEL), lambda i: (i, 0)),
        _resident((1, D_MODEL)),
        _resident(wg.shape), _resident(wu.shape), _resident(wd.shape),
        _resident((1, D_MODEL)),
        _resident(win.shape),
        _resident((QK_COLS, LANES)),
    ]
    sel = (np.arange(QK_COLS)[:, None] // HEAD_DQK == np.arange(LANES)[None, :])
    row_spec = pl.BlockSpec((tm, D_MODEL), lambda i: (i, 0))
    if transposed_heads:
        assert seq % tm == 0
        tpb = seq // tm
        batch = rows // seq
        head_t = pl.BlockSpec((1, N_HEADS, LANES, tm), lambda i: (i // tpb, 0, 0, i % tpb))
        head_n = pl.BlockSpec((1, N_HEADS, tm, LANES), lambda i: (i // tpb, 0, i % tpb, 0))
        out_specs = [row_spec, head_t, head_n, head_t,
                     pl.BlockSpec((tm, POOL_WIDTH), lambda i: (i, 0)),
                     pl.BlockSpec((1, F32_SUBLANES, LANES), lambda i: (i, 0, 0))]
        out_shape = [
            jax.ShapeDtypeStruct((rows, D_MODEL), F32),
            jax.ShapeDtypeStruct((batch, N_HEADS, LANES, seq), BF16),
            jax.ShapeDtypeStruct((batch, N_HEADS, seq, LANES), BF16),
            jax.ShapeDtypeStruct((batch, N_HEADS, LANES, seq), BF16),
            jax.ShapeDtypeStruct((rows, POOL_WIDTH), F32),
            jax.ShapeDtypeStruct((rows // tm, F32_SUBLANES, LANES), F32),
        ]
    else:
        out_specs = [row_spec, pl.BlockSpec((tm, IN_COLS), lambda i: (i, 0))]
        out_shape = [jax.ShapeDtypeStruct((rows, D_MODEL), F32),
                     jax.ShapeDtypeStruct((rows, IN_COLS), F32)]
    return pl.pallas_call(
        functools.partial(_ffn1_inproj_kernel, transposed_heads=transposed_heads),
        grid=(rows // tm,),
        in_specs=in_specs,
        out_specs=out_specs,
        out_shape=out_shape,
        scratch_shapes=[pltpu.VMEM((tm, D_MODEL), BF16), pltpu.VMEM((tm, D_MODEL), F32)],
        compiler_params=pltpu.CompilerParams(
            dimension_semantics=("parallel",), vmem_limit_bytes=VMEM_LIMIT_BYTES),
        name=name,
    )(x, g1, wg, wu, wd, gm, win, jnp.asarray(sel, BF16))


def _bias_tiles_kernel(table_ref, bdiag_ref, bmeta_ref, diag_ref, meta_ref):
    h = pl.program_id(0)
    t = ATTN_TILE

    def lookup(bucket):
        out = jnp.zeros(bucket.shape, F32)
        for b in range(N_BUCKETS):
            out = jnp.where(bucket == b, table_ref[b, h], out)
        return out

    diag_ref[0, 0] = jnp.full((t, 2 * t), table_ref[N_BUCKETS // 2 - 1, h], F32)
    diag_ref[0, N_BIAS_TILES - 1] = jnp.full((t, 2 * t), table_ref[N_BUCKETS - 1, h], F32)
    for d in range(3):
        tile = lookup(bdiag_ref[d])
        diag_ref[0, d + 1, :, 0:t] = tile
        diag_ref[0, d + 1, :, t:2 * t] = tile
    bm = bmeta_ref[...]
    near = jnp.where(bm < 0, MASK_BIAS, lookup(bm))
    far = jnp.where(bm < 0, MASK_BIAS, table_ref[N_BUCKETS // 2 - 1, h])
    for c in range(2):
        meta_ref[0, 0, :, c * t:(c + 1) * t] = near
        meta_ref[0, 1, :, c * t:(c + 1) * t] = far


def _bias_tiles(table):
    t = ATTN_TILE
    r = np.arange(t)[:, None]
    c = np.arange(t)[None, :]
    bdiag = np.stack([_rel_bucket_np((d - 1) * t + r - c) for d in range(3)])
    rm = np.arange(LANES)[:, None]
    bmeta = np.where(rm < N_META, _rel_bucket_np(rm - N_META - c), -1).astype(np.int32)
    return pl.pallas_call(
        _bias_tiles_kernel,
        grid=(N_HEADS,),
        in_specs=[
            pl.BlockSpec(memory_space=pltpu.SMEM),
            pl.BlockSpec((3, t, t), lambda h: (0, 0, 0)),
            pl.BlockSpec((LANES, t), lambda h: (0, 0)),
        ],
        out_specs=[
            pl.BlockSpec((1, N_BIAS_TILES, t, 2 * t), lambda h: (h, 0, 0, 0)),
            pl.BlockSpec((1, 2, LANES, 2 * t), lambda h: (h, 0, 0, 0)),
        ],
        out_shape=[
            jax.ShapeDtypeStruct((N_HEADS, N_BIAS_TILES, t, 2 * t), F32),
            jax.ShapeDtypeStruct((N_HEADS, 2, LANES, 2 * t), F32),
        ],
        name="bias_tiles",
    )(table, jnp.asarray(bdiag), jnp.asarray(bmeta))


def _attention_kernel(flag_ref, qt_ref, k_ref, vt_ref, km_ref, vtm_ref, bias_ref, bmeta_ref,
                      lam_ref, gain_ref, out_ref, rhs_ref, m_ref, l_ref, acc_ref, s_ref, *p_refs,
                      n_kv):
    t = ATTN_TILE
    qi = pl.program_id(1)

    for h in range(N_HEADS):
        qt = qt_ref[0, h]
        upper = lax.broadcasted_iota(jnp.int32, qt.shape, 0) < HEAD_DQK
        zero = jnp.zeros_like(qt)
        rhs_ref[h, :, 0:t] = jnp.where(upper, qt, zero)
        rhs_ref[h, :, t:2 * t] = jnp.where(upper, zero, qt)

    m_ref[...] = jnp.full(m_ref.shape, MASK_BIAS, F32)
    l_ref[...] = jnp.zeros_like(l_ref)
    acc_ref[...] = jnp.zeros_like(acc_ref)

    def key_block(j):
        start = pl.multiple_of(j * t, t)
        tile = jnp.clip(j - qi + 2, 0, N_BIAS_TILES - 1)
        return (lambda h: k_ref[0, h, pl.ds(start, t), :],
                lambda h: vt_ref[0, h, :, pl.ds(start, t)],
                lambda h: bias_ref[h, tile])

    def meta_block():
        return (lambda h: km_ref[h], lambda h: vtm_ref[h], lambda h: bmeta_ref[h, 0])

    def logits(h, kc, bias):
        return jnp.dot(kc, rhs_ref[h], preferred_element_type=F32) + bias

    def bounded_blocks():
        def weigh(h, slot, kc, bias):
            p = jnp.exp2(logits(h, kc, bias))
            p_refs[h][slot, 0:kc.shape[0]] = p.astype(BF16)
            l_ref[h] += jnp.sum(p, axis=0, keepdims=True)

        def accumulate(h, slot, vtc):
            acc_ref[h] += jnp.dot(vtc, p_refs[h][slot, 0:vtc.shape[1]],
                                  preferred_element_type=F32)

        def step(slot, block, prev_block):
            get_k, _, get_bias = block
            for h in range(N_HEADS):
                weigh(h, slot, get_k(h), get_bias(h))
                if prev_block is not None:
                    accumulate(h, 1 - slot, prev_block[1](h))

        step(0, key_block(0), None)

        def body(i, carry):
            j = 2 * i + 1
            step(1, key_block(j), key_block(j - 1))
            step(0, key_block(j + 1), key_block(j))
            return carry

        lax.fori_loop(0, n_kv // 2 - 1, body, 0, unroll=KEY_PAIR_UNROLL)
        step(1, key_block(n_kv - 1), key_block(n_kv - 2))
        step(0, meta_block(), key_block(n_kv - 1))
        for h in range(N_HEADS):
            accumulate(h, 0, vtm_ref[h])

    def general_blocks():
        def step(get_k, get_vt, get_bias):
            alphas = []
            for h in range(N_HEADS):
                kc = get_k(h)
                rows = kc.shape[0]
                s = logits(h, kc, get_bias(h))
                s_ref[h, 0:rows] = s
                m_old = m_ref[h]
                m_new = jnp.maximum(m_old, jnp.max(s, axis=0, keepdims=True))
                alphas.append(jnp.exp2(m_old - m_new))
                p = jnp.exp2(s_ref[h, 0:rows] - m_new)
                p_refs[h][0, 0:rows] = p.astype(BF16)
                l_ref[h] = alphas[h] * l_ref[h] + jnp.sum(p, axis=0, keepdims=True)
                m_ref[h] = m_new
            for h in range(N_HEADS):
                vtc = get_vt(h)
                acc_ref[h] = alphas[h] * acc_ref[h] + jnp.dot(
                    vtc, p_refs[h][0, 0:vtc.shape[1]], preferred_element_type=F32)

        step(*meta_block())

        def body(j, carry):
            step(*key_block(j))
            return carry

        lax.fori_loop(0, n_kv, body, 0)

    bounded = flag_ref[pl.program_id(0), qi] != 0
    pl.when(bounded)(bounded_blocks)
    pl.when(jnp.logical_not(bounded))(general_blocks)

    lp = lam_ref[...]
    lam = (jnp.exp(jnp.sum(lp[0:1] * lp[1:2], axis=-1, keepdims=True))
           - jnp.exp(jnp.sum(lp[2:3] * lp[3:4], axis=-1, keepdims=True)) + LAMBDA_INIT)
    for h in range(N_HEADS):
        l = l_ref[h]
        acc = acc_ref[h]
        o = acc[:, 0:t] / l[:, 0:t] - lam * (acc[:, t:2 * t] / l[:, t:2 * t])
        inv = lax.rsqrt(jnp.mean(o * o, axis=0, keepdims=True) + EPS)
        y = o * inv * gain_ref[...] * (1.0 - LAMBDA_INIT)
        out_ref[0, :, h * LANES:(h + 1) * LANES] = y.T.astype(BF16)


def _attention(bounded, qt, k, vt, k_meta, vt_meta, bias, bmeta, lam_params, gain_col, *, name):
    batch, _, _, seq = qt.shape
    t = ATTN_TILE
    assert seq % (2 * t) == 0
    n_kv = seq // t
    return pl.pallas_call(
        functools.partial(_attention_kernel, n_kv=n_kv),
        grid=(batch, n_kv),
        in_specs=[
            pl.BlockSpec(memory_space=pltpu.SMEM),
            pl.BlockSpec((1, N_HEADS, LANES, t), lambda b, q: (b, 0, 0, q)),
            pl.BlockSpec((1, N_HEADS, seq, LANES), lambda b, q: (b, 0, 0, 0),
                         pipeline_mode=pl.Buffered(1)),
            pl.BlockSpec((1, N_HEADS, LANES, seq), lambda b, q: (b, 0, 0, 0),
                         pipeline_mode=pl.Buffered(1)),
            _resident(k_meta.shape),
            _resident(vt_meta.shape),
            _resident(bias.shape),
            pl.BlockSpec((N_HEADS, 1, LANES, 2 * t), lambda b, q: (0, jnp.minimum(q, 1), 0, 0)),
            _resident((4, HEAD_DQK)),
            _resident((HEAD_DV, 1)),
        ],
        out_specs=pl.BlockSpec((1, t, ATTN_WIDTH), lambda b, q: (b, q, 0)),
        out_shape=jax.ShapeDtypeStruct((batch, seq, ATTN_WIDTH), BF16),
        scratch_shapes=[
            pltpu.VMEM((N_HEADS, LANES, 2 * t), BF16),
            pltpu.VMEM((N_HEADS, 1, 2 * t), F32),
            pltpu.VMEM((N_HEADS, 1, 2 * t), F32),
            pltpu.VMEM((N_HEADS, HEAD_DV, 2 * t), F32),
            pltpu.VMEM((N_HEADS, t, 2 * t), F32),
        ] + [pltpu.VMEM((2, t, 2 * t), BF16) for _ in range(N_HEADS)],
        compiler_params=pltpu.CompilerParams(
            dimension_semantics=("parallel", "arbitrary"),
            vmem_limit_bytes=VMEM_LIMIT_BYTES,
        ),
        name=name,
    )(bounded, qt, k, vt, k_meta, vt_meta, bias, bmeta, lam_params, gain_col)


def _bounded_flags(stats, k_meta_sq, v_meta_max, table, *, batch, seq):
    tiles = seq // ROW_TILE
    groups = 2 * N_HEADS
    q_sq = stats[:, 0, :groups].reshape(batch, tiles, groups)
    k_sq = stats[:, 1, :groups].reshape(batch, tiles, groups).max(axis=1)
    k_sq = jnp.maximum(k_sq, k_meta_sq[None, :])
    v_max = jnp.maximum(stats[:, 2, 0].reshape(batch, tiles).max(axis=1), v_meta_max)
    logit_bound = (jnp.sqrt(q_sq * k_sq[:, None, :]).max(axis=-1) * NORM_MARGIN
                   + jnp.max(jnp.abs(table)))
    ok = jnp.logical_and(logit_bound < BOUNDED_LOGIT_LIMIT,
                         (v_max < BOUNDED_VALUE_LIMIT)[:, None])
    return jnp.repeat(ok.astype(jnp.int32), ROW_TILE // ATTN_TILE, axis=1)


def _mix_ffn2_kernel(h1_ref, attn_ref, pool_ref, prev_ref, next_ref, mhalo_ref, pw_ref, ps_ref,
                     wout_ref, g2_ref, wg_ref, wu_ref, wd_ref, gf_ref, y_ref,
                     x_ref, xn_ref, acc_ref, *, tiles_per_seq, seq):
    tm = h1_ref.shape[0]
    halo = POOL_HALO
    tile = pl.program_id(0) % tiles_per_seq
    x_ref[0:halo, :] = jnp.where(tile == 0, mhalo_ref[...], prev_ref[...])
    x_ref[halo:halo + tm, :] = pool_ref[...]
    x_ref[halo + tm:, :] = jnp.where(tile == tiles_per_seq - 1, 0.0, next_ref[...])

    remaining = seq - (tile * tm + lax.broadcasted_iota(jnp.int32, (tm, 1), 0))
    mix = jnp.dot(attn_ref[...], wout_ref[0:ATTN_WIDTH, :], preferred_element_type=F32)
    for g, w in enumerate(POOL_WINDOWS):
        cols = slice(g * POOL_GROUP, (g + 1) * POOL_GROUP)
        total = x_ref[halo - w // 2:halo - w // 2 + tm, cols]
        for o in range(-w // 2 + 1, w // 2):
            total = total + x_ref[halo + o:halo + o + tm, cols]
        cnt = jnp.minimum(w, remaining + w // 2).astype(F32)
        pooled = (total / cnt - x_ref[halo:halo + tm, cols]).astype(BF16)
        mixed = jnp.dot(pooled, pw_ref[g], preferred_element_type=F32) * ps_ref[:, cols]
        mix = mix + jnp.dot(mixed.astype(BF16),
                            wout_ref[ATTN_WIDTH + g * POOL_GROUP:ATTN_WIDTH + (g + 1) * POOL_GROUP, :],
                            preferred_element_type=F32)
    h2 = h1_ref[...] + mix
    xn_ref[...] = _rms(h2, g2_ref[...]).astype(BF16)
    _swiglu_into(acc_ref, xn_ref, wg_ref, wu_ref, wd_ref)
    y_ref[...] = _rms(h2 + 0.5 * acc_ref[...], gf_ref[...])


def _mix_ffn2(h1, attn, pool, meta_halo, pw, ps, wout, g2, wg, wu, wd, gf, *, seq, name):
    rows = h1.shape[0]
    tm = ROW_TILE
    halo = POOL_HALO
    assert rows % tm == 0 and seq % tm == 0 and tm % halo == 0
    halo_blocks = rows // halo
    per_tile = tm // halo
    return pl.pallas_call(
        functools.partial(_mix_ffn2_kernel, tiles_per_seq=seq // tm, seq=seq),
        grid=(rows // tm,),
        in_specs=[
            pl.BlockSpec((tm, D_MODEL), lambda i: (i, 0)),
            pl.BlockSpec((tm, ATTN_WIDTH), lambda i: (i, 0)),
            pl.BlockSpec((tm, POOL_WIDTH), lambda i: (i, 0)),
            pl.BlockSpec((halo, POOL_WIDTH), lambda i: (jnp.maximum(i * per_tile - 1, 0), 0)),
            pl.BlockSpec((halo, POOL_WIDTH),
                         lambda i: (jnp.minimum((i + 1) * per_tile, halo_blocks - 1), 0)),
            _resident((halo, POOL_WIDTH)),
            _resident(pw.shape),
            _resident((1, POOL_WIDTH)),
            _resident(wout.shape),
            _resident((1, D_MODEL)),
            _resident(wg.shape), _resident(wu.shape), _resident(wd.shape),
            _resident((1, D_MODEL)),
        ],
        out_specs=pl.BlockSpec((tm, D_MODEL), lambda i: (i, 0)),
        out_shape=jax.ShapeDtypeStruct((rows, D_MODEL), F32),
        scratch_shapes=[
            pltpu.VMEM((tm + 2 * halo, POOL_WIDTH), F32),
            pltpu.VMEM((tm, D_MODEL), BF16),
            pltpu.VMEM((tm, D_MODEL), F32),
        ],
        compiler_params=pltpu.CompilerParams(
            dimension_semantics=("parallel",), vmem_limit_bytes=VMEM_LIMIT_BYTES),
        name=name,
    )(h1, attn, pool, pool, pool, meta_halo, pw, ps, wout, g2, wg, wu, wd, gf)


def _chunk_cols(w):
    return w.reshape(D_MODEL, N_FF_CHUNKS, FF_CHUNK).transpose(1, 0, 2).astype(BF16)


def _chunk_rows(w):
    return w.reshape(N_FF_CHUNKS, FF_CHUNK, D_MODEL).astype(BF16)


def kernel(x_prompt, x_sample, meta_tokens, rel_bias_table, norm_ffn1, ffn1_w_gate, ffn1_w_up,
           ffn1_w_down, norm_mix, w_in, lambda_q1, lambda_k1, lambda_q2, lambda_k2, subln_gain,
           pool_w, pool_scale, w_out, norm_ffn2, ffn2_w_gate, ffn2_w_up, ffn2_w_down, norm_final):
    layer = 0
    g1 = norm_ffn1[layer].reshape(1, D_MODEL)
    gm = norm_mix[layer].reshape(1, D_MODEL)
    g2 = norm_ffn2[layer].reshape(1, D_MODEL)
    gf = norm_final.reshape(1, D_MODEL)
    f1 = (_chunk_cols(ffn1_w_gate[layer]), _chunk_cols(ffn1_w_up[layer]),
          _chunk_rows(ffn1_w_down[layer]))
    f2 = (_chunk_cols(ffn2_w_gate[layer]), _chunk_cols(ffn2_w_up[layer]),
          _chunk_rows(ffn2_w_down[layer]))
    win = w_in[layer].astype(BF16)
    wout = w_out[layer].astype(BF16)
    pw = pool_w[layer].astype(BF16)
    ps = pool_scale[layer].reshape(1, POOL_WIDTH)
    lam_params = jnp.stack([lambda_q1[layer], lambda_k1[layer], lambda_q2[layer],
                            lambda_k2[layer]]).astype(F32)
    gain_col = subln_gain[layer].reshape(HEAD_DV, 1)
    table = rel_bias_table.astype(F32) * LOG2E

    _, u_meta = _ffn1_inproj(meta_tokens.astype(F32), g1, *f1, gm, win, seq=None,
                             name="ffn1_inproj_meta")
    pad = LANES - N_META
    k_meta = u_meta[:, QK_COLS:2 * QK_COLS].reshape(N_META, N_HEADS, LANES)
    k_meta = jnp.pad(k_meta.transpose(1, 0, 2), ((0, 0), (0, pad), (0, 0))).astype(BF16)
    v_meta = u_meta[:, 2 * QK_COLS:2 * QK_COLS + ATTN_WIDTH].reshape(N_META, N_HEADS, HEAD_DV)
    vt_meta = jnp.pad(v_meta.transpose(1, 2, 0), ((0, 0), (0, 0), (0, pad))).astype(BF16)
    meta_halo = u_meta[N_META - POOL_HALO:, 2 * QK_COLS + ATTN_WIDTH:]
    k_meta_sq = jnp.square(u_meta[:, QK_COLS:2 * QK_COLS]).reshape(
        N_META, 2 * N_HEADS, HEAD_DQK).sum(axis=-1).max(axis=0)
    v_meta_max = jnp.max(jnp.abs(v_meta))

    bias, bmeta = _bias_tiles(table)

    def encode(x, tag):
        batch, seq, _ = x.shape
        rows = batch * seq
        h1, qt, k, vt, pool, stats = _ffn1_inproj(x.reshape(rows, D_MODEL), g1, *f1, gm, win,
                                                  seq=seq, name="ffn1_inproj_" + tag)
        bounded = _bounded_flags(stats, k_meta_sq, v_meta_max, table, batch=batch, seq=seq)
        attn = _attention(bounded, qt, k, vt, k_meta, vt_meta, bias, bmeta, lam_params, gain_col,
                          name="attention_" + tag)
        y = _mix_ffn2(h1, attn.reshape(rows, ATTN_WIDTH), pool, meta_halo, pw, ps, wout, g2, *f2,
                      gf, seq=seq, name="mix_ffn2_" + tag)
        return y.reshape(batch, seq, D_MODEL)

    return (encode(x_prompt, "prompt"), encode(x_sample, "sample"))
```

```python
import functools
import math

import numpy as np
import jax
import jax.numpy as jnp
from jax import lax
from jax.experimental import pallas as pl
from jax.experimental.pallas import tpu as pltpu

D_MODEL = 1024
N_META = 16
D_FF = 2816
N_HEADS = 4
HEAD_DV = 128
HEAD_DQK = 64
QK_COLS = N_HEADS * 2 * HEAD_DQK
ATTN_WIDTH = N_HEADS * HEAD_DV
POOL_WINDOWS = (2, 4, 8, 16)
POOL_GROUP = 128
POOL_WIDTH = len(POOL_WINDOWS) * POOL_GROUP
IN_COLS = 2 * QK_COLS + ATTN_WIDTH + POOL_WIDTH
N_BUCKETS = 32
MAX_DISTANCE = 128
EPS = 1e-6
LAMBDA_INIT = 0.8 - 0.6 * math.exp(-0.3 * 0)
LOG2E = math.log2(math.e)
QK_SCALE_LOG2 = HEAD_DQK ** -0.5 * LOG2E

LANES = 128
F32_SUBLANES = 8
MXU_DIM = 256
VMEM_LIMIT_BYTES = 56 * 1024 * 1024

ROW_TILE = 512
FF_CHUNK = MXU_DIM
N_FF_CHUNKS = D_FF // FF_CHUNK
ATTN_TILE = 256
POOL_HALO = max(POOL_WINDOWS) // 2
MASK_BIAS = -1e30
BIAS_SATURATION_DISTANCE = 91
KEY_PAIR_UNROLL = 5
FF_CHUNK_UNROLL = True
N_BIAS_TILES = 5

BOUNDED_LOGIT_LIMIT = 48.0
BOUNDED_VALUE_LIMIT = 1e15
NORM_MARGIN = 1.05

assert D_FF % FF_CHUNK == 0
assert POOL_HALO == F32_SUBLANES and N_META >= POOL_HALO
assert ATTN_TILE >= BIAS_SATURATION_DISTANCE

F32 = jnp.float32
BF16 = jnp.bfloat16


def _rel_bucket_np(rel):
    half = N_BUCKETS // 2
    max_exact = half // 2
    ret = np.where(rel > 0, half, 0)
    n = np.abs(rel)
    nf = np.maximum(n, 1).astype(np.float64)
    large = max_exact + (np.log(nf / max_exact) / math.log(MAX_DISTANCE / max_exact)
                         * (half - max_exact)).astype(np.int32)
    large = np.minimum(large, half - 1)
    return (ret + np.where(n < max_exact, n, large)).astype(np.int32)


def _check_saturation():
    n = np.arange(BIAS_SATURATION_DISTANCE, 1 << 15)
    assert np.all(_rel_bucket_np(-n) == N_BUCKETS // 2 - 1)
    assert np.all(_rel_bucket_np(n) == N_BUCKETS - 1)


_check_saturation()


def _rms(x, gain):
    return x * lax.rsqrt(jnp.mean(x * x, axis=-1, keepdims=True) + EPS) * gain


def _swiglu_into(acc_ref, xn_ref, wg_ref, wu_ref, wd_ref):
    acc_ref[...] = jnp.zeros_like(acc_ref)

    def body(j, carry):
        xn = xn_ref[...]
        g = jnp.dot(xn, wg_ref[j], preferred_element_type=F32)
        u = jnp.dot(xn, wu_ref[j], preferred_element_type=F32)
        a = (g * (1.0 / (1.0 + jnp.exp(-g))) * u).astype(BF16)
        acc_ref[...] += jnp.dot(a, wd_ref[j], preferred_element_type=F32)
        return carry

    lax.fori_loop(0, N_FF_CHUNKS, body, 0, unroll=FF_CHUNK_UNROLL)


def _ffn1_inproj_kernel(x_ref, g1_ref, wg_ref, wu_ref, wd_ref, gm_ref, win_ref, sel_ref, *refs,
                        transposed_heads):
    if transposed_heads:
        h1_ref, qt_ref, k_ref, vt_ref, pool_ref, stat_ref, xn_ref, acc_ref = refs
    else:
        h1_ref, u_ref, xn_ref, acc_ref = refs
    x = x_ref[...]
    xn_ref[...] = _rms(x, g1_ref[...]).astype(BF16)
    _swiglu_into(acc_ref, xn_ref, wg_ref, wu_ref, wd_ref)
    h1 = x + 0.5 * acc_ref[...]
    h1_ref[...] = h1
    hn = _rms(h1, gm_ref[...]).astype(BF16)
    if not transposed_heads:
        u_ref[...] = jnp.dot(hn, win_ref[...], preferred_element_type=F32)
        return
    uq = jnp.dot(hn, win_ref[:, 0:QK_COLS], preferred_element_type=F32) * QK_SCALE_LOG2
    for h in range(N_HEADS):
        qt_ref[0, h] = uq[:, h * LANES:(h + 1) * LANES].T.astype(BF16)
    uk = jnp.dot(hn, win_ref[:, QK_COLS:2 * QK_COLS], preferred_element_type=F32)
    for h in range(N_HEADS):
        k_ref[0, h] = uk[:, h * LANES:(h + 1) * LANES].astype(BF16)
    uv = jnp.dot(hn, win_ref[:, 2 * QK_COLS:2 * QK_COLS + ATTN_WIDTH],
                 preferred_element_type=F32)
    for h in range(N_HEADS):
        vt_ref[0, h] = uv[:, h * LANES:(h + 1) * LANES].T.astype(BF16)
    pool_ref[...] = jnp.dot(hn, win_ref[:, 2 * QK_COLS + ATTN_WIDTH:],
                            preferred_element_type=F32)

    def max_sq_norms(u):
        sq = jnp.dot((u * u).astype(BF16), sel_ref[...], preferred_element_type=F32)
        return jnp.max(sq, axis=0, keepdims=True)

    stat_ref[0, 0:1, :] = max_sq_norms(uq)
    stat_ref[0, 1:2, :] = max_sq_norms(uk)
    stat_ref[0, 2:3, :] = jnp.broadcast_to(jnp.max(jnp.abs(uv), keepdims=True), (1, LANES))
    stat_ref[0, 3:, :] = jnp.zeros((F32_SUBLANES - 3, LANES), F32)


def _resident(shape):
    zeros = (0,) * len(shape)
    return pl.BlockSpec(shape, lambda *_: zeros, pipeline_mode=pl.Buffered(1))


def _ffn1_inproj(x, g1, wg, wu, wd, gm, win, *, seq, name):
    rows = x.shape[0]
    transposed_heads = seq is not None
    tm = ROW_TILE if transposed_heads else rows
    assert rows % tm == 0
    in_specs = [
        pl.BlockSpec((tm, D_MODEL), lambda i: (i, 0)),
        _resident((1, D_MODEL)),
        _resident(wg.shape), _resident(wu.shape), _resident(wd.shape),
        _resident((1, D_MODEL)),
        _resident(win.shape),
        _resident((QK_COLS, LANES)),
    ]
    sel = (np.arange(QK_COLS)[:, None] // HEAD_DQK == np.arange(LANES)[None, :])
    row_spec = pl.BlockSpec((tm, D_MODEL), lambda i: (i, 0))
    if transposed_heads:
        assert seq % tm == 0
        tpb = seq // tm
        batch = rows // seq
        head_t = pl.BlockSpec((1, N_HEADS, LANES, tm), lambda i: (i // tpb, 0, 0, i % tpb))
        head_n = pl.BlockSpec((1, N_HEADS, tm, LANES), lambda i: (i // tpb, 0, i % tpb, 0))
        out_specs = [row_spec, head_t, head_n, head_t,
                     pl.BlockSpec((tm, POOL_WIDTH), lambda i: (i, 0)),
                     pl.BlockSpec((1, F32_SUBLANES, LANES), lambda i: (i, 0, 0))]
        out_shape = [
            jax.ShapeDtypeStruct((rows, D_MODEL), F32),
            jax.ShapeDtypeStruct((batch, N_HEADS, LANES, seq), BF16),
            jax.ShapeDtypeStruct((batch, N_HEADS, seq, LANES), BF16),
            jax.ShapeDtypeStruct((batch, N_HEADS, LANES, seq), BF16),
            jax.ShapeDtypeStruct((rows, POOL_WIDTH), F32),
            jax.ShapeDtypeStruct((rows // tm, F32_SUBLANES, LANES), F32),
        ]
    else:
        out_specs = [row_spec, pl.BlockSpec((tm, IN_COLS), lambda i: (i, 0))]
        out_shape = [jax.ShapeDtypeStruct((rows, D_MODEL), F32),
                     jax.ShapeDtypeStruct((rows, IN_COLS), F32)]
    return pl.pallas_call(
        functools.partial(_ffn1_inproj_kernel, transposed_heads=transposed_heads),
        grid=(rows // tm,),
        in_specs=in_specs,
        out_specs=out_specs,
        out_shape=out_shape,
        scratch_shapes=[pltpu.VMEM((tm, D_MODEL), BF16), pltpu.VMEM((tm, D_MODEL), F32)],
        compiler_params=pltpu.CompilerParams(
            dimension_semantics=("parallel",), vmem_limit_bytes=VMEM_LIMIT_BYTES),
        name=name,
    )(x, g1, wg, wu, wd, gm, win, jnp.asarray(sel, BF16))


def _bias_tiles_kernel(table_ref, bdiag_ref, bmeta_ref, diag_ref, meta_ref):
    h = pl.program_id(0)
    t = ATTN_TILE

    def lookup(bucket):
        out = jnp.zeros(bucket.shape, F32)
        for b in range(N_BUCKETS):
            out = jnp.where(bucket == b, table_ref[b, h], out)
        return out

    diag_ref[0, 0] = jnp.full((t, 2 * t), table_ref[N_BUCKETS // 2 - 1, h], F32)
    diag_ref[0, N_BIAS_TILES - 1] = jnp.full((t, 2 * t), table_ref[N_BUCKETS - 1, h], F32)
    for d in range(3):
        tile = lookup(bdiag_ref[d])
        diag_ref[0, d + 1, :, 0:t] = tile
        diag_ref[0, d + 1, :, t:2 * t] = tile
    bm = bmeta_ref[...]
    near = jnp.where(bm < 0, MASK_BIAS, lookup(bm))
    far = jnp.where(bm < 0, MASK_BIAS, table_ref[N_BUCKETS // 2 - 1, h])
    for c in range(2):
        meta_ref[0, 0, :, c * t:(c + 1) * t] = near
        meta_ref[0, 1, :, c * t:(c + 1) * t] = far


def _bias_tiles(table):
    t = ATTN_TILE
    r = np.arange(t)[:, None]
    c = np.arange(t)[None, :]
    bdiag = np.stack([_rel_bucket_np((d - 1) * t + r - c) for d in range(3)])
    rm = np.arange(LANES)[:, None]
    bmeta = np.where(rm < N_META, _rel_bucket_np(rm - N_META - c), -1).astype(np.int32)
    return pl.pallas_call(
        _bias_tiles_kernel,
        grid=(N_HEADS,),
        in_specs=[
            pl.BlockSpec(memory_space=pltpu.SMEM),
            pl.BlockSpec((3, t, t), lambda h: (0, 0, 0)),
            pl.BlockSpec((LANES, t), lambda h: (0, 0)),
        ],
        out_specs=[
            pl.BlockSpec((1, N_BIAS_TILES, t, 2 * t), lambda h: (h, 0, 0, 0)),
            pl.BlockSpec((1, 2, LANES, 2 * t), lambda h: (h, 0, 0, 0)),
        ],
        out_shape=[
            jax.ShapeDtypeStruct((N_HEADS, N_BIAS_TILES, t, 2 * t), F32),
            jax.ShapeDtypeStruct((N_HEADS, 2, LANES, 2 * t), F32),
        ],
        name="bias_tiles",
    )(table, jnp.asarray(bdiag), jnp.asarray(bmeta))


def _attention_kernel(flag_ref, qt_ref, k_ref, vt_ref, km_ref, vtm_ref, bias_ref, bmeta_ref,
                      lam_ref, gain_ref, out_ref, rhs_ref, m_ref, l_ref, acc_ref, s_ref, *p_refs,
                      n_kv):
    t = ATTN_TILE
    qi = pl.program_id(1)

    for h in range(N_HEADS):
        qt = qt_ref[0, h]
        upper = lax.broadcasted_iota(jnp.int32, qt.shape, 0) < HEAD_DQK
        zero = jnp.zeros_like(qt)
        rhs_ref[h, :, 0:t] = jnp.where(upper, qt, zero)
        rhs_ref[h, :, t:2 * t] = jnp.where(upper, zero, qt)

    m_ref[...] = jnp.full(m_ref.shape, MASK_BIAS, F32)
    l_ref[...] = jnp.zeros_like(l_ref)
    acc_ref[...] = jnp.zeros_like(acc_ref)

    def key_block(j):
        start = pl.multiple_of(j * t, t)
        tile = jnp.clip(j - qi + 2, 0, N_BIAS_TILES - 1)
        return (lambda h: k_ref[0, h, pl.ds(start, t), :],
                lambda h: vt_ref[0, h, :, pl.ds(start, t)],
                lambda h: bias_ref[h, tile])

    def meta_block():
        return (lambda h: km_ref[h], lambda h: vtm_ref[h], lambda h: bmeta_ref[h, 0])

    def logits(h, kc, bias):
        return jnp.dot(kc, rhs_ref[h], preferred_element_type=F32) + bias

    def bounded_blocks():
        def weigh(h, slot, kc, bias):
            p = jnp.exp2(logits(h, kc, bias))
            p_refs[h][slot, 0:kc.shape[0]] = p.astype(BF16)
            l_ref[h] += jnp.sum(p, axis=0, keepdims=True)

        def accumulate(h, slot, vtc):
            acc_ref[h] += jnp.dot(vtc, p_refs[h][slot, 0:vtc.shape[1]],
                                  preferred_element_type=F32)

        def step(slot, block, prev_block):
            get_k, _, get_bias = block
            for h in range(N_HEADS):
                weigh(h, slot, get_k(h), get_bias(h))
                if prev_block is not None:
                    accumulate(h, 1 - slot, prev_block[1](h))

        step(0, key_block(0), None)

        def body(i, carry):
            j = 2 * i + 1
            step(1, key_block(j), key_block(j - 1))
            step(0, key_block(j + 1), key_block(j))
            return carry

        lax.fori_loop(0, n_kv // 2 - 1, body, 0, unroll=KEY_PAIR_UNROLL)
        step(1, key_block(n_kv - 1), key_block(n_kv - 2))
        step(0, meta_block(), key_block(n_kv - 1))
        for h in range(N_HEADS):
            accumulate(h, 0, vtm_ref[h])

    def general_blocks():
        def step(get_k, get_vt, get_bias):
            alphas = []
            for h in range(N_HEADS):
                kc = get_k(h)
                rows = kc.shape[0]
                s = logits(h, kc, get_bias(h))
                s_ref[h, 0:rows] = s
                m_old = m_ref[h]
                m_new = jnp.maximum(m_old, jnp.max(s, axis=0, keepdims=True))
                alphas.append(jnp.exp2(m_old - m_new))
                p = jnp.exp2(s_ref[h, 0:rows] - m_new)
                p_refs[h][0, 0:rows] = p.astype(BF16)
                l_ref[h] = alphas[h] * l_ref[h] + jnp.sum(p, axis=0, keepdims=True)
                m_ref[h] = m_new
            for h in range(N_HEADS):
                vtc = get_vt(h)
                acc_ref[h] = alphas[h] * acc_ref[h] + jnp.dot(
                    vtc, p_refs[h][0, 0:vtc.shape[1]], preferred_element_type=F32)

        step(*meta_block())

        def body(j, carry):
            step(*key_block(j))
            return carry

        lax.fori_loop(0, n_kv, body, 0)

    bounded = flag_ref[pl.program_id(0), qi] != 0
    pl.when(bounded)(bounded_blocks)
    pl.when(jnp.logical_not(bounded))(general_blocks)

    lp = lam_ref[...]
    lam = (jnp.exp(jnp.sum(lp[0:1] * lp[1:2], axis=-1, keepdims=True))
           - jnp.exp(jnp.sum(lp[2:3] * lp[3:4], axis=-1, keepdims=True)) + LAMBDA_INIT)
    for h in range(N_HEADS):
        l = l_ref[h]
        acc = acc_ref[h]
        o = acc[:, 0:t] / l[:, 0:t] - lam * (acc[:, t:2 * t] / l[:, t:2 * t])
        inv = lax.rsqrt(jnp.mean(o * o, axis=0, keepdims=True) + EPS)
        y = o * inv * gain_ref[...] * (1.0 - LAMBDA_INIT)
        out_ref[0, :, h * LANES:(h + 1) * LANES] = y.T.astype(BF16)


def _attention(bounded, qt, k, vt, k_meta, vt_meta, bias, bmeta, lam_params, gain_col, *, name):
    batch, _, _, seq = qt.shape
    t = ATTN_TILE
    assert seq % (2 * t) == 0
    n_kv = seq // t
    return pl.pallas_call(
        functools.partial(_attention_kernel, n_kv=n_kv),
        grid=(batch, n_kv),
        in_specs=[
            pl.BlockSpec(memory_space=pltpu.SMEM),
            pl.BlockSpec((1, N_HEADS, LANES, t), lambda b, q: (b, 0, 0, q)),
            pl.BlockSpec((1, N_HEADS, seq, LANES), lambda b, q: (b, 0, 0, 0)),
            pl.BlockSpec((1, N_HEADS, LANES, seq), lambda b, q: (b, 0, 0, 0)),
            _resident(k_meta.shape),
            _resident(vt_meta.shape),
            _resident(bias.shape),
            pl.BlockSpec((N_HEADS, 1, LANES, 2 * t), lambda b, q: (0, jnp.minimum(q, 1), 0, 0)),
            _resident((4, HEAD_DQK)),
            _resident((HEAD_DV, 1)),
        ],
        out_specs=pl.BlockSpec((1, t, ATTN_WIDTH), lambda b, q: (b, q, 0)),
        out_shape=jax.ShapeDtypeStruct((batch, seq, ATTN_WIDTH), BF16),
        scratch_shapes=[
            pltpu.VMEM((N_HEADS, LANES, 2 * t), BF16),
            pltpu.VMEM((N_HEADS, 1, 2 * t), F32),
            pltpu.VMEM((N_HEADS, 1, 2 * t), F32),
            pltpu.VMEM((N_HEADS, HEAD_DV, 2 * t), F32),
            pltpu.VMEM((N_HEADS, t, 2 * t), F32),
        ] + [pltpu.VMEM((2, t, 2 * t), BF16) for _ in range(N_HEADS)],
        compiler_params=pltpu.CompilerParams(
            dimension_semantics=("parallel", "arbitrary"),
            vmem_limit_bytes=VMEM_LIMIT_BYTES,
        ),
        name=name,
    )(bounded, qt, k, vt, k_meta, vt_meta, bias, bmeta, lam_params, gain_col)


def _bounded_flags(stats, k_meta_sq, v_meta_max, table, *, batch, seq):
    tiles = seq // ROW_TILE
    groups = 2 * N_HEADS
    q_sq = stats[:, 0, :groups].reshape(batch, tiles, groups)
    k_sq = stats[:, 1, :groups].reshape(batch, tiles, groups).max(axis=1)
    k_sq = jnp.maximum(k_sq, k_meta_sq[None, :])
    v_max = jnp.maximum(stats[:, 2, 0].reshape(batch, tiles).max(axis=1), v_meta_max)
    logit_bound = (jnp.sqrt(q_sq * k_sq[:, None, :]).max(axis=-1) * NORM_MARGIN
                   + jnp.max(jnp.abs(table)))
    ok = jnp.logical_and(logit_bound < BOUNDED_LOGIT_LIMIT,
                         (v_max < BOUNDED_VALUE_LIMIT)[:, None])
    return jnp.repeat(ok.astype(jnp.int32), ROW_TILE // ATTN_TILE, axis=1)


def _mix_ffn2_kernel(h1_ref, attn_ref, pool_ref, prev_ref, next_ref, mhalo_ref, pw_ref, ps_ref,
                     wout_ref, g2_ref, wg_ref, wu_ref, wd_ref, gf_ref, y_ref,
                     x_ref, xn_ref, acc_ref, *, tiles_per_seq, seq):
    tm = h1_ref.shape[0]
    halo = POOL_HALO
    tile = pl.program_id(0) % tiles_per_seq
    x_ref[0:halo, :] = jnp.where(tile == 0, mhalo_ref[...], prev_ref[...])
    x_ref[halo:halo + tm, :] = pool_ref[...]
    x_ref[halo + tm:, :] = jnp.where(tile == tiles_per_seq - 1, 0.0, next_ref[...])

    remaining = seq - (tile * tm + lax.broadcasted_iota(jnp.int32, (tm, 1), 0))
    xn_ref[:, 0:ATTN_WIDTH] = attn_ref[...]
    for g, w in enumerate(POOL_WINDOWS):
        cols = slice(g * POOL_GROUP, (g + 1) * POOL_GROUP)
        total = x_ref[halo - w // 2:halo - w // 2 + tm, cols]
        for o in range(-w // 2 + 1, w // 2):
            total = total + x_ref[halo + o:halo + o + tm, cols]
        cnt = jnp.minimum(w, remaining + w // 2).astype(F32)
        pooled = (total / cnt - x_ref[halo:halo + tm, cols]).astype(BF16)
        mixed = jnp.dot(pooled, pw_ref[g], preferred_element_type=F32) * ps_ref[:, cols]
        xn_ref[:, ATTN_WIDTH + g * POOL_GROUP:ATTN_WIDTH + (g + 1) * POOL_GROUP] = (
            mixed.astype(BF16))
    h2 = h1_ref[...] + jnp.dot(xn_ref[...], wout_ref[...], preferred_element_type=F32)
    xn_ref[...] = _rms(h2, g2_ref[...]).astype(BF16)
    _swiglu_into(acc_ref, xn_ref, wg_ref, wu_ref, wd_ref)
    y_ref[...] = _rms(h2 + 0.5 * acc_ref[...], gf_ref[...])


def _mix_ffn2(h1, attn, pool, meta_halo, pw, ps, wout, g2, wg, wu, wd, gf, *, seq, name):
    rows = h1.shape[0]
    tm = ROW_TILE
    halo = POOL_HALO
    assert rows % tm == 0 and seq % tm == 0 and tm % halo == 0
    halo_blocks = rows // halo
    per_tile = tm // halo
    return pl.pallas_call(
        functools.partial(_mix_ffn2_kernel, tiles_per_seq=seq // tm, seq=seq),
        grid=(rows // tm,),
        in_specs=[
            pl.BlockSpec((tm, D_MODEL), lambda i: (i, 0)),
            pl.BlockSpec((tm, ATTN_WIDTH), lambda i: (i, 0)),
            pl.BlockSpec((tm, POOL_WIDTH), lambda i: (i, 0)),
            pl.BlockSpec((halo, POOL_WIDTH), lambda i: (jnp.maximum(i * per_tile - 1, 0), 0)),
            pl.BlockSpec((halo, POOL_WIDTH),
                         lambda i: (jnp.minimum((i + 1) * per_tile, halo_blocks - 1), 0)),
            _resident((halo, POOL_WIDTH)),
            _resident(pw.shape),
            _resident((1, POOL_WIDTH)),
            _resident(wout.shape),
            _resident((1, D_MODEL)),
            _resident(wg.shape), _resident(wu.shape), _resident(wd.shape),
            _resident((1, D_MODEL)),
        ],
        out_specs=pl.BlockSpec((tm, D_MODEL), lambda i: (i, 0)),
        out_shape=jax.ShapeDtypeStruct((rows, D_MODEL), F32),
        scratch_shapes=[
            pltpu.VMEM((tm + 2 * halo, POOL_WIDTH), F32),
            pltpu.VMEM((tm, D_MODEL), BF16),
            pltpu.VMEM((tm, D_MODEL), F32),
        ],
        compiler_params=pltpu.CompilerParams(
            dimension_semantics=("parallel",), vmem_limit_bytes=VMEM_LIMIT_BYTES),
        name=name,
    )(h1, attn, pool, pool, pool, meta_halo, pw, ps, wout, g2, wg, wu, wd, gf)


def _chunk_cols(w):
    return w.reshape(D_MODEL, N_FF_CHUNKS, FF_CHUNK).transpose(1, 0, 2).astype(BF16)


def _chunk_rows(w):
    return w.reshape(N_FF_CHUNKS, FF_CHUNK, D_MODEL).astype(BF16)


def kernel(x_prompt, x_sample, meta_tokens, rel_bias_table, norm_ffn1, ffn1_w_gate, ffn1_w_up,
           ffn1_w_down, norm_mix, w_in, lambda_q1, lambda_k1, lambda_q2, lambda_k2, subln_gain,
           pool_w, pool_scale, w_out, norm_ffn2, ffn2_w_gate, ffn2_w_up, ffn2_w_down, norm_final):
    layer = 0
    g1 = norm_ffn1[layer].reshape(1, D_MODEL)
    gm = norm_mix[layer].reshape(1, D_MODEL)
    g2 = norm_ffn2[layer].reshape(1, D_MODEL)
    gf = norm_final.reshape(1, D_MODEL)
    f1 = (_chunk_cols(ffn1_w_gate[layer]), _chunk_cols(ffn1_w_up[layer]),
          _chunk_rows(ffn1_w_down[layer]))
    f2 = (_chunk_cols(ffn2_w_gate[layer]), _chunk_cols(ffn2_w_up[layer]),
          _chunk_rows(ffn2_w_down[layer]))
    win = w_in[layer].astype(BF16)
    wout = w_out[layer].astype(BF16)
    pw = pool_w[layer].astype(BF16)
    ps = pool_scale[layer].reshape(1, POOL_WIDTH)
    lam_params = jnp.stack([lambda_q1[layer], lambda_k1[layer], lambda_q2[layer],
                            lambda_k2[layer]]).astype(F32)
    gain_col = subln_gain[layer].reshape(HEAD_DV, 1)
    table = rel_bias_table.astype(F32) * LOG2E

    _, u_meta = _ffn1_inproj(meta_tokens.astype(F32), g1, *f1, gm, win, seq=None,
                             name="ffn1_inproj_meta")
    pad = LANES - N_META
    k_meta = u_meta[:, QK_COLS:2 * QK_COLS].reshape(N_META, N_HEADS, LANES)
    k_meta = jnp.pad(k_meta.transpose(1, 0, 2), ((0, 0), (0, pad), (0, 0))).astype(BF16)
    v_meta = u_meta[:, 2 * QK_COLS:2 * QK_COLS + ATTN_WIDTH].reshape(N_META, N_HEADS, HEAD_DV)
    vt_meta = jnp.pad(v_meta.transpose(1, 2, 0), ((0, 0), (0, 0), (0, pad))).astype(BF16)
    meta_halo = u_meta[N_META - POOL_HALO:, 2 * QK_COLS + ATTN_WIDTH:]
    k_meta_sq = jnp.square(u_meta[:, QK_COLS:2 * QK_COLS]).reshape(
        N_META, 2 * N_HEADS, HEAD_DQK).sum(axis=-1).max(axis=0)
    v_meta_max = jnp.max(jnp.abs(v_meta))

    bias, bmeta = _bias_tiles(table)

    def encode(x, tag):
        batch, seq, _ = x.shape
        rows = batch * seq
        h1, qt, k, vt, pool, stats = _ffn1_inproj(x.reshape(rows, D_MODEL), g1, *f1, gm, win,
                                                  seq=seq, name="ffn1_inproj_" + tag)
        bounded = _bounded_flags(stats, k_meta_sq, v_meta_max, table, batch=batch, seq=seq)
        attn = _attention(bounded, qt, k, vt, k_meta, vt_meta, bias, bmeta, lam_params, gain_col,
                          name="attention_" + tag)
        y = _mix_ffn2(h1, attn.reshape(rows, ATTN_WIDTH), pool, meta_halo, pw, ps, wout, g2, *f2,
                      gf, seq=seq, name="mix_ffn2_" + tag)
        return y.reshape(batch, seq, D_MODEL)

    return (encode(x_prompt, "prompt"), encode(x_sample, "sample"))
```

```python
import functools
import math

import numpy as np
import jax
import jax.numpy as jnp
from jax import lax
from jax.experimental import pallas as pl
from jax.experimental.pallas import tpu as pltpu

D_MODEL = 1024
N_META = 16
D_FF = 2816
N_HEADS = 4
HEAD_DV = 128
HEAD_DQK = 64
QK_COLS = N_HEADS * 2 * HEAD_DQK
ATTN_WIDTH = N_HEADS * HEAD_DV
POOL_WINDOWS = (2, 4, 8, 16)
POOL_GROUP = 128
POOL_WIDTH = len(POOL_WINDOWS) * POOL_GROUP
IN_COLS = 2 * QK_COLS + ATTN_WIDTH + POOL_WIDTH
N_BUCKETS = 32
MAX_DISTANCE = 128
EPS = 1e-6
LAMBDA_INIT = 0.8 - 0.6 * math.exp(-0.3 * 0)
LOG2E = math.log2(math.e)
QK_SCALE_LOG2 = HEAD_DQK ** -0.5 * LOG2E

LANES = 128
F32_SUBLANES = 8
MXU_DIM = 256
VMEM_LIMIT_BYTES = 56 * 1024 * 1024

ROW_TILE = 512
FF_CHUNK = MXU_DIM
N_FF_CHUNKS = D_FF // FF_CHUNK
ATTN_TILE = 256
POOL_HALO = max(POOL_WINDOWS) // 2
MASK_BIAS = -1e30
BIAS_SATURATION_DISTANCE = 91
KEY_PAIR_UNROLL = 4
N_BIAS_TILES = 5

BOUNDED_LOGIT_LIMIT = 48.0
BOUNDED_VALUE_LIMIT = 1e15
NORM_MARGIN = 1.05

assert D_FF % FF_CHUNK == 0
assert POOL_HALO == F32_SUBLANES and N_META >= POOL_HALO
assert ATTN_TILE >= BIAS_SATURATION_DISTANCE

F32 = jnp.float32
BF16 = jnp.bfloat16


def _rel_bucket_np(rel):
    half = N_BUCKETS // 2
    max_exact = half // 2
    ret = np.where(rel > 0, half, 0)
    n = np.abs(rel)
    nf = np.maximum(n, 1).astype(np.float64)
    large = max_exact + (np.log(nf / max_exact) / math.log(MAX_DISTANCE / max_exact)
                         * (half - max_exact)).astype(np.int32)
    large = np.minimum(large, half - 1)
    return (ret + np.where(n < max_exact, n, large)).astype(np.int32)


def _check_saturation():
    n = np.arange(BIAS_SATURATION_DISTANCE, 1 << 15)
    assert np.all(_rel_bucket_np(-n) == N_BUCKETS // 2 - 1)
    assert np.all(_rel_bucket_np(n) == N_BUCKETS - 1)


_check_saturation()


def _rms(x, gain):
    return x * lax.rsqrt(jnp.mean(x * x, axis=-1, keepdims=True) + EPS) * gain


def _swiglu_into(acc_ref, xn_ref, wg_ref, wu_ref, wd_ref):
    acc_ref[...] = jnp.zeros_like(acc_ref)
    for j in range(N_FF_CHUNKS):
        xn = xn_ref[...]
        g = jnp.dot(xn, wg_ref[j], preferred_element_type=F32)
        u = jnp.dot(xn, wu_ref[j], preferred_element_type=F32)
        a = (g * (1.0 / (1.0 + jnp.exp(-g))) * u).astype(BF16)
        acc_ref[...] += jnp.dot(a, wd_ref[j], preferred_element_type=F32)


def _ffn1_inproj_kernel(x_ref, g1_ref, wg_ref, wu_ref, wd_ref, gm_ref, win_ref, sel_ref, *refs,
                        transposed_heads):
    if transposed_heads:
        h1_ref, qt_ref, k_ref, vt_ref, pool_ref, stat_ref, xn_ref, acc_ref = refs
    else:
        h1_ref, u_ref, xn_ref, acc_ref = refs
    x = x_ref[...]
    xn_ref[...] = _rms(x, g1_ref[...]).astype(BF16)
    _swiglu_into(acc_ref, xn_ref, wg_ref, wu_ref, wd_ref)
    h1 = x + 0.5 * acc_ref[...]
    h1_ref[...] = h1
    hn = _rms(h1, gm_ref[...]).astype(BF16)
    if not transposed_heads:
        u_ref[...] = jnp.dot(hn, win_ref[...], preferred_element_type=F32)
        return
    uq = jnp.dot(hn, win_ref[:, 0:QK_COLS], preferred_element_type=F32) * QK_SCALE_LOG2
    for h in range(N_HEADS):
        qt_ref[0, h] = uq[:, h * LANES:(h + 1) * LANES].T.astype(BF16)
    uk = jnp.dot(hn, win_ref[:, QK_COLS:2 * QK_COLS], preferred_element_type=F32)
    for h in range(N_HEADS):
        k_ref[0, h] = uk[:, h * LANES:(h + 1) * LANES].astype(BF16)
    uv = jnp.dot(hn, win_ref[:, 2 * QK_COLS:2 * QK_COLS + ATTN_WIDTH],
                 preferred_element_type=F32)
    for h in range(N_HEADS):
        vt_ref[0, h] = uv[:, h * LANES:(h + 1) * LANES].T.astype(BF16)
    pool_ref[...] = jnp.dot(hn, win_ref[:, 2 * QK_COLS + ATTN_WIDTH:],
                            preferred_element_type=F32)

    def max_sq_norms(u):
        sq = jnp.dot((u * u).astype(BF16), sel_ref[...], preferred_element_type=F32)
        return jnp.max(sq, axis=0, keepdims=True)

    stat_ref[0, 0:1, :] = max_sq_norms(uq)
    stat_ref[0, 1:2, :] = max_sq_norms(uk)
    stat_ref[0, 2:3, :] = jnp.broadcast_to(jnp.max(jnp.abs(uv), keepdims=True), (1, LANES))
    stat_ref[0, 3:, :] = jnp.zeros((F32_SUBLANES - 3, LANES), F32)


def _resident(shape):
    zeros = (0,) * len(shape)
    return pl.BlockSpec(shape, lambda *_: zeros, pipeline_mode=pl.Buffered(1))


def _ffn1_inproj(x, g1, wg, wu, wd, gm, win, *, seq, name):
    rows = x.shape[0]
    transposed_heads = seq is not None
    tm = ROW_TILE if transposed_heads else rows
    assert rows % tm == 0
    in_specs = [
        pl.BlockSpec((tm, D_MODEL), lambda i: (i, 0)),
        _resident((1, D_MODEL)),
        _resident(wg.shape), _resident(wu.shape), _resident(wd.shape),
        _resident((1, D_MODEL)),
        _resident(win.shape),
        _resident((QK_COLS, LANES)),
    ]
    sel = (np.arange(QK_COLS)[:, None] // HEAD_DQK == np.arange(LANES)[None, :])
    row_spec = pl.BlockSpec((tm, D_MODEL), lambda i: (i, 0))
    if transposed_heads:
        assert seq % tm == 0
        tpb = seq // tm
        batch = rows // seq
        head_t = pl.BlockSpec((1, N_HEADS, LANES, tm), lambda i: (i // tpb, 0, 0, i % tpb))
        head_n = pl.BlockSpec((1, N_HEADS, tm, LANES), lambda i: (i // tpb, 0, i % tpb, 0))
        out_specs = [row_spec, head_t, head_n, head_t,
                     pl.BlockSpec((tm, POOL_WIDTH), lambda i: (i, 0)),
                     pl.BlockSpec((1, F32_SUBLANES, LANES), lambda i: (i, 0, 0))]
        out_shape = [
            jax.ShapeDtypeStruct((rows, D_MODEL), F32),
            jax.ShapeDtypeStruct((batch, N_HEADS, LANES, seq), BF16),
            jax.ShapeDtypeStruct((batch, N_HEADS, seq, LANES), BF16),
            jax.ShapeDtypeStruct((batch, N_HEADS, LANES, seq), BF16),
            jax.ShapeDtypeStruct((rows, POOL_WIDTH), F32),
            jax.ShapeDtypeStruct((rows // tm, F32_SUBLANES, LANES), F32),
        ]
    else:
        out_specs = [row_spec, pl.BlockSpec((tm, IN_COLS), lambda i: (i, 0))]
        out_shape = [jax.ShapeDtypeStruct((rows, D_MODEL), F32),
                     jax.ShapeDtypeStruct((rows, IN_COLS), F32)]
    return pl.pallas_call(
        functools.partial(_ffn1_inproj_kernel, transposed_heads=transposed_heads),
        grid=(rows // tm,),
        in_specs=in_specs,
        out_specs=out_specs,
        out_shape=out_shape,
        scratch_shapes=[pltpu.VMEM((tm, D_MODEL), BF16), pltpu.VMEM((tm, D_MODEL), F32)],
        compiler_params=pltpu.CompilerParams(
            dimension_semantics=("parallel",), vmem_limit_bytes=VMEM_LIMIT_BYTES),
        name=name,
    )(x, g1, wg, wu, wd, gm, win, jnp.asarray(sel, BF16))


def _bias_tiles_kernel(table_ref, bdiag_ref, bmeta_ref, diag_ref, meta_ref):
    h = pl.program_id(0)
    t = ATTN_TILE

    def lookup(bucket):
        out = jnp.zeros(bucket.shape, F32)
        for b in range(N_BUCKETS):
            out = jnp.where(bucket == b, table_ref[b, h], out)
        return out

    diag_ref[0, 0] = jnp.full((t, 2 * t), table_ref[N_BUCKETS // 2 - 1, h], F32)
    diag_ref[0, N_BIAS_TILES - 1] = jnp.full((t, 2 * t), table_ref[N_BUCKETS - 1, h], F32)
    for d in range(3):
        tile = lookup(bdiag_ref[d])
        diag_ref[0, d + 1, :, 0:t] = tile
        diag_ref[0, d + 1, :, t:2 * t] = tile
    bm = bmeta_ref[...]
    near = jnp.where(bm < 0, MASK_BIAS, lookup(bm))
    far = jnp.where(bm < 0, MASK_BIAS, table_ref[N_BUCKETS // 2 - 1, h])
    for c in range(2):
        meta_ref[0, 0, :, c * t:(c + 1) * t] = near
        meta_ref[0, 1, :, c * t:(c + 1) * t] = far


def _bias_tiles(table):
    t = ATTN_TILE
    r = np.arange(t)[:, None]
    c = np.arange(t)[None, :]
    bdiag = np.stack([_rel_bucket_np((d - 1) * t + r - c) for d in range(3)])
    rm = np.arange(LANES)[:, None]
    bmeta = np.where(rm < N_META, _rel_bucket_np(rm - N_META - c), -1).astype(np.int32)
    return pl.pallas_call(
        _bias_tiles_kernel,
        grid=(N_HEADS,),
        in_specs=[
            pl.BlockSpec(memory_space=pltpu.SMEM),
            pl.BlockSpec((3, t, t), lambda h: (0, 0, 0)),
            pl.BlockSpec((LANES, t), lambda h: (0, 0)),
        ],
        out_specs=[
            pl.BlockSpec((1, N_BIAS_TILES, t, 2 * t), lambda h: (h, 0, 0, 0)),
            pl.BlockSpec((1, 2, LANES, 2 * t), lambda h: (h, 0, 0, 0)),
        ],
        out_shape=[
            jax.ShapeDtypeStruct((N_HEADS, N_BIAS_TILES, t, 2 * t), F32),
            jax.ShapeDtypeStruct((N_HEADS, 2, LANES, 2 * t), F32),
        ],
        name="bias_tiles",
    )(table, jnp.asarray(bdiag), jnp.asarray(bmeta))


def _attention_kernel(flag_ref, qt_ref, k_ref, vt_ref, km_ref, vtm_ref, bias_ref, bmeta_ref,
                      lam_ref, gain_ref, out_ref, rhs_ref, m_ref, l_ref, acc_ref, s_ref, *p_refs,
                      n_kv):
    t = ATTN_TILE
    qi = pl.program_id(1)

    for h in range(N_HEADS):
        qt = qt_ref[0, h]
        upper = lax.broadcasted_iota(jnp.int32, qt.shape, 0) < HEAD_DQK
        zero = jnp.zeros_like(qt)
        rhs_ref[h, :, 0:t] = jnp.where(upper, qt, zero)
        rhs_ref[h, :, t:2 * t] = jnp.where(upper, zero, qt)

    m_ref[...] = jnp.full(m_ref.shape, MASK_BIAS, F32)
    l_ref[...] = jnp.zeros_like(l_ref)
    acc_ref[...] = jnp.zeros_like(acc_ref)

    def key_block(j):
        start = pl.multiple_of(j * t, t)
        tile = jnp.clip(j - qi + 2, 0, N_BIAS_TILES - 1)
        return (lambda h: k_ref[0, h, pl.ds(start, t), :],
                lambda h: vt_ref[0, h, :, pl.ds(start, t)],
                lambda h: bias_ref[h, tile])

    def meta_block():
        return (lambda h: km_ref[h], lambda h: vtm_ref[h], lambda h: bmeta_ref[h, 0])

    def logits(h, kc, bias):
        return jnp.dot(kc, rhs_ref[h], preferred_element_type=F32) + bias

    def bounded_blocks():
        def weigh(h, slot, half, kc, bias):
            p = jnp.exp2(logits(h, kc, bias))
            p_refs[h][slot, half * t:half * t + kc.shape[0]] = p.astype(BF16)
            l_ref[h] += jnp.sum(p, axis=0, keepdims=True)

        def accumulate(h, slot, vtc):
            acc_ref[h] += jnp.dot(vtc, p_refs[h][slot, 0:vtc.shape[1]],
                                  preferred_element_type=F32)

        def pair_values(i):
            start = pl.multiple_of(i * 2 * t, 2 * t)
            return lambda h: vt_ref[0, h, :, pl.ds(start, 2 * t)]

        def step(slot, i, prev):
            first, second = key_block(2 * i), key_block(2 * i + 1)
            for h in range(N_HEADS):
                weigh(h, slot, 0, first[0](h), first[2](h))
                if prev is not None:
                    accumulate(h, 1 - slot, pair_values(prev)(h))
                weigh(h, slot, 1, second[0](h), second[2](h))

        n_pairs = n_kv // 2
        step(0, 0, None)

        def body(i, carry):
            j = 2 * i + 1
            step(1, j, j - 1)
            step(0, j + 1, j)
            return carry

        lax.fori_loop(0, n_pairs // 2 - 1, body, 0, unroll=KEY_PAIR_UNROLL)
        step(1, n_pairs - 1, n_pairs - 2)
        get_k, get_vt, get_bias = meta_block()
        for h in range(N_HEADS):
            weigh(h, 0, 0, get_k(h), get_bias(h))
            accumulate(h, 1, pair_values(n_pairs - 1)(h))
        for h in range(N_HEADS):
            accumulate(h, 0, get_vt(h))

    def general_blocks():
        def step(get_k, get_vt, get_bias):
            alphas = []
            for h in range(N_HEADS):
                kc = get_k(h)
                rows = kc.shape[0]
                s = logits(h, kc, get_bias(h))
                s_ref[h, 0:rows] = s
                m_old = m_ref[h]
                m_new = jnp.maximum(m_old, jnp.max(s, axis=0, keepdims=True))
                alphas.append(jnp.exp2(m_old - m_new))
                p = jnp.exp2(s_ref[h, 0:rows] - m_new)
                p_refs[h][0, 0:rows] = p.astype(BF16)
                l_ref[h] = alphas[h] * l_ref[h] + jnp.sum(p, axis=0, keepdims=True)
                m_ref[h] = m_new
            for h in range(N_HEADS):
                vtc = get_vt(h)
                acc_ref[h] = alphas[h] * acc_ref[h] + jnp.dot(
                    vtc, p_refs[h][0, 0:vtc.shape[1]], preferred_element_type=F32)

        step(*meta_block())

        def body(j, carry):
            step(*key_block(j))
            return carry

        lax.fori_loop(0, n_kv, body, 0)

    bounded = flag_ref[pl.program_id(0), qi] != 0
    pl.when(bounded)(bounded_blocks)
    pl.when(jnp.logical_not(bounded))(general_blocks)

    lp = lam_ref[...]
    lam = (jnp.exp(jnp.sum(lp[0:1] * lp[1:2], axis=-1, keepdims=True))
           - jnp.exp(jnp.sum(lp[2:3] * lp[3:4], axis=-1, keepdims=True)) + LAMBDA_INIT)
    for h in range(N_HEADS):
        l = l_ref[h]
        acc = acc_ref[h]
        o = acc[:, 0:t] / l[:, 0:t] - lam * (acc[:, t:2 * t] / l[:, t:2 * t])
        inv = lax.rsqrt(jnp.mean(o * o, axis=0, keepdims=True) + EPS)
        y = o * inv * gain_ref[...] * (1.0 - LAMBDA_INIT)
        out_ref[0, :, h * LANES:(h + 1) * LANES] = y.T.astype(BF16)


def _attention(bounded, qt, k, vt, k_meta, vt_meta, bias, bmeta, lam_params, gain_col, *, name):
    batch, _, _, seq = qt.shape
    t = ATTN_TILE
    assert seq % (4 * t) == 0
    n_kv = seq // t
    return pl.pallas_call(
        functools.partial(_attention_kernel, n_kv=n_kv),
        grid=(batch, n_kv),
        in_specs=[
            pl.BlockSpec(memory_space=pltpu.SMEM),
            pl.BlockSpec((1, N_HEADS, LANES, t), lambda b, q: (b, 0, 0, q)),
            pl.BlockSpec((1, N_HEADS, seq, LANES), lambda b, q: (b, 0, 0, 0)),
            pl.BlockSpec((1, N_HEADS, LANES, seq), lambda b, q: (b, 0, 0, 0)),
            _resident(k_meta.shape),
            _resident(vt_meta.shape),
            _resident(bias.shape),
            pl.BlockSpec((N_HEADS, 1, LANES, 2 * t), lambda b, q: (0, jnp.minimum(q, 1), 0, 0)),
            _resident((4, HEAD_DQK)),
            _resident((HEAD_DV, 1)),
        ],
        out_specs=pl.BlockSpec((1, t, ATTN_WIDTH), lambda b, q: (b, q, 0)),
        out_shape=jax.ShapeDtypeStruct((batch, seq, ATTN_WIDTH), BF16),
        scratch_shapes=[
            pltpu.VMEM((N_HEADS, LANES, 2 * t), BF16),
            pltpu.VMEM((N_HEADS, 1, 2 * t), F32),
            pltpu.VMEM((N_HEADS, 1, 2 * t), F32),
            pltpu.VMEM((N_HEADS, HEAD_DV, 2 * t), F32),
            pltpu.VMEM((N_HEADS, t, 2 * t), F32),
        ] + [pltpu.VMEM((2, 2 * t, 2 * t), BF16) for _ in range(N_HEADS)],
        compiler_params=pltpu.CompilerParams(
            dimension_semantics=("parallel", "arbitrary"),
            vmem_limit_bytes=VMEM_LIMIT_BYTES,
        ),
        name=name,
    )(bounded, qt, k, vt, k_meta, vt_meta, bias, bmeta, lam_params, gain_col)


def _bounded_flags(stats, k_meta_sq, v_meta_max, table, *, batch, seq):
    tiles = seq // ROW_TILE
    groups = 2 * N_HEADS
    q_sq = stats[:, 0, :groups].reshape(batch, tiles, groups)
    k_sq = stats[:, 1, :groups].reshape(batch, tiles, groups).max(axis=1)
    k_sq = jnp.maximum(k_sq, k_meta_sq[None, :])
    v_max = jnp.maximum(stats[:, 2, 0].reshape(batch, tiles).max(axis=1), v_meta_max)
    logit_bound = (jnp.sqrt(q_sq * k_sq[:, None, :]).max(axis=-1) * NORM_MARGIN
                   + jnp.max(jnp.abs(table)))
    ok = jnp.logical_and(logit_bound < BOUNDED_LOGIT_LIMIT,
                         (v_max < BOUNDED_VALUE_LIMIT)[:, None])
    return jnp.repeat(ok.astype(jnp.int32), ROW_TILE // ATTN_TILE, axis=1)


def _mix_ffn2_kernel(h1_ref, attn_ref, pool_ref, prev_ref, next_ref, mhalo_ref, pw_ref, ps_ref,
                     wout_ref, g2_ref, wg_ref, wu_ref, wd_ref, gf_ref, y_ref,
                     x_ref, sum2_ref, sum4_ref, sum8_ref, sum16_ref, pooled_ref, xn_ref, acc_ref,
                     *, tiles_per_seq):
    sum_refs = (sum2_ref, sum4_ref, sum8_ref, sum16_ref)
    assert tuple(2 << k for k in range(len(sum_refs))) == POOL_WINDOWS
    tm = h1_ref.shape[0]
    halo = POOL_HALO
    lead = 2 * halo
    ext = tm + 2 * halo
    tile = pl.program_id(0) % tiles_per_seq
    last_tile = tile == tiles_per_seq - 1
    x_ref[0:halo, :] = jnp.zeros((halo, POOL_WIDTH), F32)
    x_ref[halo:lead, :] = jnp.where(tile == 0, mhalo_ref[...], prev_ref[...])
    x_ref[lead:lead + tm, :] = pool_ref[...]
    x_ref[lead + tm:, :] = jnp.where(last_tile, 0.0, next_ref[...])

    src = x_ref
    for k, dst in enumerate(sum_refs):
        cols = slice(k * POOL_GROUP, POOL_WIDTH)
        reach = 1 << k
        dst[0:halo, :] = jnp.zeros((halo, POOL_WIDTH), F32)
        dst[halo:halo + ext, cols] = (src[halo:halo + ext, cols]
                                      + src[halo - reach:halo - reach + ext, cols])
        src = dst

    def pooled_rows(g, w, rows, cnt):
        cols = slice(g * POOL_GROUP, (g + 1) * POOL_GROUP)
        end = w // 2 - 1
        total = sum_refs[g][lead + rows.start + end:lead + rows.stop + end, cols]
        return (total / cnt - x_ref[lead + rows.start:lead + rows.stop, cols]).astype(BF16)

    for g, w in enumerate(POOL_WINDOWS):
        pooled_ref[:, g * POOL_GROUP:(g + 1) * POOL_GROUP] = pooled_rows(
            g, w, slice(0, tm), float(w))

    @pl.when(last_tile)
    def _():
        rows = slice(tm - halo, tm)
        remaining = halo - lax.broadcasted_iota(jnp.int32, (halo, 1), 0)
        for g, w in enumerate(POOL_WINDOWS):
            cnt = jnp.minimum(w, remaining + w // 2).astype(F32)
            pooled_ref[rows, g * POOL_GROUP:(g + 1) * POOL_GROUP] = pooled_rows(g, w, rows, cnt)

    xn_ref[:, 0:ATTN_WIDTH] = attn_ref[...]
    for g in range(len(POOL_WINDOWS)):
        cols = slice(g * POOL_GROUP, (g + 1) * POOL_GROUP)
        mixed = jnp.dot(pooled_ref[:, cols], pw_ref[g], preferred_element_type=F32) * ps_ref[:, cols]
        xn_ref[:, ATTN_WIDTH + g * POOL_GROUP:ATTN_WIDTH + (g + 1) * POOL_GROUP] = (
            mixed.astype(BF16))
    h2 = h1_ref[...] + jnp.dot(xn_ref[...], wout_ref[...], preferred_element_type=F32)
    xn_ref[...] = _rms(h2, g2_ref[...]).astype(BF16)
    _swiglu_into(acc_ref, xn_ref, wg_ref, wu_ref, wd_ref)
    y_ref[...] = _rms(h2 + 0.5 * acc_ref[...], gf_ref[...])


def _mix_ffn2(h1, attn, pool, meta_halo, pw, ps, wout, g2, wg, wu, wd, gf, *, seq, name):
    rows = h1.shape[0]
    tm = ROW_TILE
    halo = POOL_HALO
    assert rows % tm == 0 and seq % tm == 0 and tm % halo == 0
    halo_blocks = rows // halo
    per_tile = tm // halo
    sums = pltpu.VMEM((tm + 3 * halo, POOL_WIDTH), F32)
    return pl.pallas_call(
        functools.partial(_mix_ffn2_kernel, tiles_per_seq=seq // tm),
        grid=(rows // tm,),
        in_specs=[
            pl.BlockSpec((tm, D_MODEL), lambda i: (i, 0)),
            pl.BlockSpec((tm, ATTN_WIDTH), lambda i: (i, 0)),
            pl.BlockSpec((tm, POOL_WIDTH), lambda i: (i, 0)),
            pl.BlockSpec((halo, POOL_WIDTH), lambda i: (jnp.maximum(i * per_tile - 1, 0), 0)),
            pl.BlockSpec((halo, POOL_WIDTH),
                         lambda i: (jnp.minimum((i + 1) * per_tile, halo_blocks - 1), 0)),
            _resident((halo, POOL_WIDTH)),
            _resident(pw.shape),
            _resident((1, POOL_WIDTH)),
            _resident(wout.shape),
            _resident((1, D_MODEL)),
            _resident(wg.shape), _resident(wu.shape), _resident(wd.shape),
            _resident((1, D_MODEL)),
        ],
        out_specs=pl.BlockSpec((tm, D_MODEL), lambda i: (i, 0)),
        out_shape=jax.ShapeDtypeStruct((rows, D_MODEL), F32),
        scratch_shapes=[
            sums, sums, sums, sums, sums,
            pltpu.VMEM((tm, POOL_WIDTH), BF16),
            pltpu.VMEM((tm, D_MODEL), BF16),
            pltpu.VMEM((tm, D_MODEL), F32),
        ],
        compiler_params=pltpu.CompilerParams(
            dimension_semantics=("parallel",), vmem_limit_bytes=VMEM_LIMIT_BYTES),
        name=name,
    )(h1, attn, pool, pool, pool, meta_halo, pw, ps, wout, g2, wg, wu, wd, gf)


def _chunk_cols(w):
    return w.reshape(D_MODEL, N_FF_CHUNKS, FF_CHUNK).transpose(1, 0, 2).astype(BF16)


def _chunk_rows(w):
    return w.reshape(N_FF_CHUNKS, FF_CHUNK, D_MODEL).astype(BF16)


def kernel(x_prompt, x_sample, meta_tokens, rel_bias_table, norm_ffn1, ffn1_w_gate, ffn1_w_up,
           ffn1_w_down, norm_mix, w_in, lambda_q1, lambda_k1, lambda_q2, lambda_k2, subln_gain,
           pool_w, pool_scale, w_out, norm_ffn2, ffn2_w_gate, ffn2_w_up, ffn2_w_down, norm_final):
    layer = 0
    g1 = norm_ffn1[layer].reshape(1, D_MODEL)
    gm = norm_mix[layer].reshape(1, D_MODEL)
    g2 = norm_ffn2[layer].reshape(1, D_MODEL)
    gf = norm_final.reshape(1, D_MODEL)
    f1 = (_chunk_cols(ffn1_w_gate[layer]), _chunk_cols(ffn1_w_up[layer]),
          _chunk_rows(ffn1_w_down[layer]))
    f2 = (_chunk_cols(ffn2_w_gate[layer]), _chunk_cols(ffn2_w_up[layer]),
          _chunk_rows(ffn2_w_down[layer]))
    win = w_in[layer].astype(BF16)
    wout = w_out[layer].astype(BF16)
    pw = pool_w[layer].astype(BF16)
    ps = pool_scale[layer].reshape(1, POOL_WIDTH)
    lam_params = jnp.stack([lambda_q1[layer], lambda_k1[layer], lambda_q2[layer],
                            lambda_k2[layer]]).astype(F32)
    gain_col = subln_gain[layer].reshape(HEAD_DV, 1)
    table = rel_bias_table.astype(F32) * LOG2E

    _, u_meta = _ffn1_inproj(meta_tokens.astype(F32), g1, *f1, gm, win, seq=None,
                             name="ffn1_inproj_meta")
    pad = LANES - N_META
    k_meta = u_meta[:, QK_COLS:2 * QK_COLS].reshape(N_META, N_HEADS, LANES)
    k_meta = jnp.pad(k_meta.transpose(1, 0, 2), ((0, 0), (0, pad), (0, 0))).astype(BF16)
    v_meta = u_meta[:, 2 * QK_COLS:2 * QK_COLS + ATTN_WIDTH].reshape(N_META, N_HEADS, HEAD_DV)
    vt_meta = jnp.pad(v_meta.transpose(1, 2, 0), ((0, 0), (0, 0), (0, pad))).astype(BF16)
    meta_halo = u_meta[N_META - POOL_HALO:, 2 * QK_COLS + ATTN_WIDTH:]
    k_meta_sq = jnp.square(u_meta[:, QK_COLS:2 * QK_COLS]).reshape(
        N_META, 2 * N_HEADS, HEAD_DQK).sum(axis=-1).max(axis=0)
    v_meta_max = jnp.max(jnp.abs(v_meta))

    bias, bmeta = _bias_tiles(table)

    def encode(x, tag):
        batch, seq, _ = x.shape
        rows = batch * seq
        h1, qt, k, vt, pool, stats = _ffn1_inproj(x.reshape(rows, D_MODEL), g1, *f1, gm, win,
                                                  seq=seq, name="ffn1_inproj_" + tag)
        bounded = _bounded_flags(stats, k_meta_sq, v_meta_max, table, batch=batch, seq=seq)
        attn = _attention(bounded, qt, k, vt, k_meta, vt_meta, bias, bmeta, lam_params, gain_col,
                          name="attention_" + tag)
        y = _mix_ffn2(h1, attn.reshape(rows, ATTN_WIDTH), pool, meta_halo, pw, ps, wout, g2, *f2,
                      gf, seq=seq, name="mix_ffn2_" + tag)
        return y.reshape(batch, seq, D_MODEL)

    return (encode(x_prompt, "prompt"), encode(x_sample, "sample"))
```

```python
import functools
import math

import numpy as np
import jax
import jax.numpy as jnp
from jax import lax
from jax.experimental import pallas as pl
from jax.experimental.pallas import tpu as pltpu

D_MODEL = 1024
N_META = 16
D_FF = 2816
N_HEADS = 4
HEAD_DV = 128
HEAD_DQK = 64
QK_COLS = N_HEADS * 2 * HEAD_DQK
ATTN_WIDTH = N_HEADS * HEAD_DV
POOL_WINDOWS = (2, 4, 8, 16)
POOL_GROUP = 128
POOL_WIDTH = len(POOL_WINDOWS) * POOL_GROUP
IN_COLS = 2 * QK_COLS + ATTN_WIDTH + POOL_WIDTH
N_BUCKETS = 32
MAX_DISTANCE = 128
EPS = 1e-6
LAMBDA_INIT = 0.8 - 0.6 * math.exp(-0.3 * 0)
LOG2E = math.log2(math.e)
QK_SCALE_LOG2 = HEAD_DQK ** -0.5 * LOG2E

LANES = 128
F32_SUBLANES = 8
MXU_DIM = 256
VMEM_LIMIT_BYTES = 56 * 1024 * 1024

ROW_TILE = 512
FF_CHUNK = MXU_DIM
N_FF_CHUNKS = D_FF // FF_CHUNK
ATTN_TILE = 256
POOL_HALO = max(POOL_WINDOWS) // 2
MASK_BIAS = -1e30
BIAS_SATURATION_DISTANCE = 91
KEY_PAIR_UNROLL = 4
N_BIAS_TILES = 5

BOUNDED_LOGIT_LIMIT = 48.0
BOUNDED_VALUE_LIMIT = 1e15
NORM_MARGIN = 1.05

assert D_FF % FF_CHUNK == 0
assert POOL_HALO == F32_SUBLANES and N_META >= POOL_HALO
assert ATTN_TILE >= BIAS_SATURATION_DISTANCE

F32 = jnp.float32
BF16 = jnp.bfloat16


def _rel_bucket_np(rel):
    half = N_BUCKETS // 2
    max_exact = half // 2
    ret = np.where(rel > 0, half, 0)
    n = np.abs(rel)
    nf = np.maximum(n, 1).astype(np.float64)
    large = max_exact + (np.log(nf / max_exact) / math.log(MAX_DISTANCE / max_exact)
                         * (half - max_exact)).astype(np.int32)
    large = np.minimum(large, half - 1)
    return (ret + np.where(n < max_exact, n, large)).astype(np.int32)


def _check_saturation():
    n = np.arange(BIAS_SATURATION_DISTANCE, 1 << 15)
    assert np.all(_rel_bucket_np(-n) == N_BUCKETS // 2 - 1)
    assert np.all(_rel_bucket_np(n) == N_BUCKETS - 1)


_check_saturation()


def _rms(x, gain):
    return x * lax.rsqrt(jnp.mean(x * x, axis=-1, keepdims=True) + EPS) * gain


def _swiglu_into(acc_ref, xn_ref, wg_ref, wu_ref, wd_ref):
    acc_ref[...] = jnp.zeros_like(acc_ref)
    for j in range(N_FF_CHUNKS):
        xn = xn_ref[...]
        g = jnp.dot(xn, wg_ref[j], preferred_element_type=F32)
        u = jnp.dot(xn, wu_ref[j], preferred_element_type=F32)
        a = (g * (1.0 / (1.0 + jnp.exp(-g))) * u).astype(BF16)
        acc_ref[...] += jnp.dot(a, wd_ref[j], preferred_element_type=F32)


def _ffn1_inproj_kernel(x_ref, g1_ref, wg_ref, wu_ref, wd_ref, gm_ref, win_ref, sel_ref, *refs,
                        transposed_heads):
    if transposed_heads:
        h1_ref, qt_ref, k_ref, vt_ref, pool_ref, stat_ref, xn_ref, acc_ref = refs
    else:
        h1_ref, u_ref, xn_ref, acc_ref = refs
    x = x_ref[...]
    xn_ref[...] = _rms(x, g1_ref[...]).astype(BF16)
    _swiglu_into(acc_ref, xn_ref, wg_ref, wu_ref, wd_ref)
    h1 = x + 0.5 * acc_ref[...]
    h1_ref[...] = h1
    hn = _rms(h1, gm_ref[...]).astype(BF16)
    if not transposed_heads:
        u_ref[...] = jnp.dot(hn, win_ref[...], preferred_element_type=F32)
        return
    uq = jnp.dot(hn, win_ref[:, 0:QK_COLS], preferred_element_type=F32) * QK_SCALE_LOG2
    for h in range(N_HEADS):
        qt_ref[0, h] = uq[:, h * LANES:(h + 1) * LANES].T.astype(BF16)
    uk = jnp.dot(hn, win_ref[:, QK_COLS:2 * QK_COLS], preferred_element_type=F32)
    for h in range(N_HEADS):
        k_ref[0, h] = uk[:, h * LANES:(h + 1) * LANES].astype(BF16)
    uv = jnp.dot(hn, win_ref[:, 2 * QK_COLS:2 * QK_COLS + ATTN_WIDTH],
                 preferred_element_type=F32)
    for h in range(N_HEADS):
        vt_ref[0, h] = uv[:, h * LANES:(h + 1) * LANES].T.astype(BF16)
    pool_ref[...] = jnp.dot(hn, win_ref[:, 2 * QK_COLS + ATTN_WIDTH:],
                            preferred_element_type=F32)

    def max_sq_norms(u):
        sq = jnp.dot((u * u).astype(BF16), sel_ref[...], preferred_element_type=F32)
        return jnp.max(sq, axis=0, keepdims=True)

    stat_ref[0, 0:1, :] = max_sq_norms(uq)
    stat_ref[0, 1:2, :] = max_sq_norms(uk)
    stat_ref[0, 2:3, :] = jnp.broadcast_to(jnp.max(jnp.abs(uv), keepdims=True), (1, LANES))
    stat_ref[0, 3:, :] = jnp.zeros((F32_SUBLANES - 3, LANES), F32)


def _resident(shape):
    zeros = (0,) * len(shape)
    return pl.BlockSpec(shape, lambda *_: zeros, pipeline_mode=pl.Buffered(1))


def _ffn1_inproj(x, g1, wg, wu, wd, gm, win, *, seq, name):
    rows = x.shape[0]
    transposed_heads = seq is not None
    tm = ROW_TILE if transposed_heads else rows
    assert rows % tm == 0
    in_specs = [
        pl.BlockSpec((tm, D_MODEL), lambda i: (i, 0)),
        _resident((1, D_MODEL)),
        _resident(wg.shape), _resident(wu.shape), _resident(wd.shape),
        _resident((1, D_MODEL)),
        _resident(win.shape),
        _resident((QK_COLS, LANES)),
    ]
    sel = (np.arange(QK_COLS)[:, None] // HEAD_DQK == np.arange(LANES)[None, :])
    row_spec = pl.BlockSpec((tm, D_MODEL), lambda i: (i, 0))
    if transposed_heads:
        assert seq % tm == 0
        tpb = seq // tm
        batch = rows // seq
        head_t = pl.BlockSpec((1, N_HEADS, LANES, tm), lambda i: (i // tpb, 0, 0, i % tpb))
        head_n = pl.BlockSpec((1, N_HEADS, tm, LANES), lambda i: (i // tpb, 0, i % tpb, 0))
        out_specs = [row_spec, head_t, head_n, head_t,
                     pl.BlockSpec((tm, POOL_WIDTH), lambda i: (i, 0)),
                     pl.BlockSpec((1, F32_SUBLANES, LANES), lambda i: (i, 0, 0))]
        out_shape = [
            jax.ShapeDtypeStruct((rows, D_MODEL), F32),
            jax.ShapeDtypeStruct((batch, N_HEADS, LANES, seq), BF16),
            jax.ShapeDtypeStruct((batch, N_HEADS, seq, LANES), BF16),
            jax.ShapeDtypeStruct((batch, N_HEADS, LANES, seq), BF16),
            jax.ShapeDtypeStruct((rows, POOL_WIDTH), F32),
            jax.ShapeDtypeStruct((rows // tm, F32_SUBLANES, LANES), F32),
        ]
    else:
        out_specs = [row_spec, pl.BlockSpec((tm, IN_COLS), lambda i: (i, 0))]
        out_shape = [jax.ShapeDtypeStruct((rows, D_MODEL), F32),
                     jax.ShapeDtypeStruct((rows, IN_COLS), F32)]
    return pl.pallas_call(
        functools.partial(_ffn1_inproj_kernel, transposed_heads=transposed_heads),
        grid=(rows // tm,),
        in_specs=in_specs,
        out_specs=out_specs,
        out_shape=out_shape,
        scratch_shapes=[pltpu.VMEM((tm, D_MODEL), BF16), pltpu.VMEM((tm, D_MODEL), F32)],
        compiler_params=pltpu.CompilerParams(
            dimension_semantics=("parallel",), vmem_limit_bytes=VMEM_LIMIT_BYTES),
        name=name,
    )(x, g1, wg, wu, wd, gm, win, jnp.asarray(sel, BF16))


def _bias_tiles_kernel(table_ref, bdiag_ref, bmeta_ref, diag_ref, meta_ref):
    h = pl.program_id(0)
    t = ATTN_TILE

    def lookup(bucket):
        out = jnp.zeros(bucket.shape, F32)
        for b in range(N_BUCKETS):
            out = jnp.where(bucket == b, table_ref[b, h], out)
        return out

    diag_ref[0, 0] = jnp.full((t, 2 * t), table_ref[N_BUCKETS // 2 - 1, h], F32)
    diag_ref[0, N_BIAS_TILES - 1] = jnp.full((t, 2 * t), table_ref[N_BUCKETS - 1, h], F32)
    for d in range(3):
        tile = lookup(bdiag_ref[d])
        diag_ref[0, d + 1, :, 0:t] = tile
        diag_ref[0, d + 1, :, t:2 * t] = tile
    bm = bmeta_ref[...]
    near = jnp.where(bm < 0, MASK_BIAS, lookup(bm))
    far = jnp.where(bm < 0, MASK_BIAS, table_ref[N_BUCKETS // 2 - 1, h])
    for c in range(2):
        meta_ref[0, 0, :, c * t:(c + 1) * t] = near
        meta_ref[0, 1, :, c * t:(c + 1) * t] = far


def _bias_tiles(table):
    t = ATTN_TILE
    r = np.arange(t)[:, None]
    c = np.arange(t)[None, :]
    bdiag = np.stack([_rel_bucket_np((d - 1) * t + r - c) for d in range(3)])
    rm = np.arange(LANES)[:, None]
    bmeta = np.where(rm < N_META, _rel_bucket_np(rm - N_META - c), -1).astype(np.int32)
    return pl.pallas_call(
        _bias_tiles_kernel,
        grid=(N_HEADS,),
        in_specs=[
            pl.BlockSpec(memory_space=pltpu.SMEM),
            pl.BlockSpec((3, t, t), lambda h: (0, 0, 0)),
            pl.BlockSpec((LANES, t), lambda h: (0, 0)),
        ],
        out_specs=[
            pl.BlockSpec((1, N_BIAS_TILES, t, 2 * t), lambda h: (h, 0, 0, 0)),
            pl.BlockSpec((1, 2, LANES, 2 * t), lambda h: (h, 0, 0, 0)),
        ],
        out_shape=[
            jax.ShapeDtypeStruct((N_HEADS, N_BIAS_TILES, t, 2 * t), F32),
            jax.ShapeDtypeStruct((N_HEADS, 2, LANES, 2 * t), F32),
        ],
        name="bias_tiles",
    )(table, jnp.asarray(bdiag), jnp.asarray(bmeta))


def _attention_kernel(flag_ref, table_ref, qt_ref, k_ref, vt_ref, km_ref, vtm_ref, bias_ref, bmeta_ref,
                      lam_ref, gain_ref, out_ref, rhs_ref, m_ref, l_ref, acc_ref, s_ref, *p_refs,
                      n_kv):
    t = ATTN_TILE
    qi = pl.program_id(1)

    for h in range(N_HEADS):
        qt = qt_ref[0, h]
        upper = lax.broadcasted_iota(jnp.int32, qt.shape, 0) < HEAD_DQK
        zero = jnp.zeros_like(qt)
        rhs_ref[h, :, 0:t] = jnp.where(upper, qt, zero)
        rhs_ref[h, :, t:2 * t] = jnp.where(upper, zero, qt)

    m_ref[...] = jnp.full(m_ref.shape, MASK_BIAS, F32)
    l_ref[...] = jnp.zeros_like(l_ref)
    acc_ref[...] = jnp.zeros_like(acc_ref)

    def key_block(j, saturated=False):
        start = pl.multiple_of(j * t, t)
        if saturated:
            row = jnp.where(j > qi, N_BUCKETS - 1, N_BUCKETS // 2 - 1)
            get_bias = lambda h: table_ref[row, h]
        else:
            tile = jnp.clip(j - qi + 2, 0, N_BIAS_TILES - 1)
            get_bias = lambda h: bias_ref[h, tile]
        return (lambda h: k_ref[0, h, pl.ds(start, t), :],
                lambda h: vt_ref[0, h, :, pl.ds(start, t)],
                get_bias)

    def meta_block():
        return (lambda h: km_ref[h], lambda h: vtm_ref[h], lambda h: bmeta_ref[h, 0])

    def logits(h, kc, bias):
        return jnp.dot(kc, rhs_ref[h], preferred_element_type=F32) + bias

    def bounded_blocks():
        def weigh(h, slot, half, kc, bias):
            p = jnp.exp2(logits(h, kc, bias))
            p_refs[h][slot, half * t:half * t + kc.shape[0]] = p.astype(BF16)
            l_ref[h] += jnp.sum(p, axis=0, keepdims=True)

        def accumulate(h, slot, vtc):
            acc_ref[h] += jnp.dot(vtc, p_refs[h][slot, 0:vtc.shape[1]],
                                  preferred_element_type=F32)

        n_pairs = n_kv // 2
        first_block = (qi + n_kv - 1) & -2
        first_block = jnp.where(first_block >= n_kv, first_block - n_kv, first_block)

        def pair_start(i):
            j = first_block + 2 * i
            return jnp.where(j >= n_kv, j - n_kv, j)

        def pair_values(i):
            start = pl.multiple_of(pair_start(i) * t, 2 * t)
            return lambda h: vt_ref[0, h, :, pl.ds(start, 2 * t)]

        def step(slot, i, prev, saturated=True):
            j = pair_start(i)
            first, second = key_block(j, saturated), key_block(j + 1, saturated)
            for h in range(N_HEADS):
                weigh(h, slot, 0, first[0](h), first[2](h))
                if prev is not None:
                    accumulate(h, 1 - slot, pair_values(prev)(h))
                weigh(h, slot, 1, second[0](h), second[2](h))

        step(0, 0, None, saturated=False)
        step(1, 1, 0, saturated=False)

        def body(m, carry):
            i = 2 * m + 2
            step(0, i, i - 1)
            step(1, i + 1, i)
            return carry

        lax.fori_loop(0, n_pairs // 2 - 1, body, 0, unroll=KEY_PAIR_UNROLL)
        get_k, get_vt, get_bias = meta_block()
        for h in range(N_HEADS):
            weigh(h, 0, 0, get_k(h), get_bias(h))
            accumulate(h, 1, pair_values(n_pairs - 1)(h))
        for h in range(N_HEADS):
            accumulate(h, 0, get_vt(h))

    def general_blocks():
        def step(get_k, get_vt, get_bias):
            alphas = []
            for h in range(N_HEADS):
                kc = get_k(h)
                rows = kc.shape[0]
                s = logits(h, kc, get_bias(h))
                s_ref[h, 0:rows] = s
                m_old = m_ref[h]
                m_new = jnp.maximum(m_old, jnp.max(s, axis=0, keepdims=True))
                alphas.append(jnp.exp2(m_old - m_new))
                p = jnp.exp2(s_ref[h, 0:rows] - m_new)
                p_refs[h][0, 0:rows] = p.astype(BF16)
                l_ref[h] = alphas[h] * l_ref[h] + jnp.sum(p, axis=0, keepdims=True)
                m_ref[h] = m_new
            for h in range(N_HEADS):
                vtc = get_vt(h)
                acc_ref[h] = alphas[h] * acc_ref[h] + jnp.dot(
                    vtc, p_refs[h][0, 0:vtc.shape[1]], preferred_element_type=F32)

        step(*meta_block())

        def body(j, carry):
            step(*key_block(j))
            return carry

        lax.fori_loop(0, n_kv, body, 0)

    bounded = flag_ref[pl.program_id(0), qi] != 0
    pl.when(bounded)(bounded_blocks)
    pl.when(jnp.logical_not(bounded))(general_blocks)

    lp = lam_ref[...]
    lam = (jnp.exp(jnp.sum(lp[0:1] * lp[1:2], axis=-1, keepdims=True))
           - jnp.exp(jnp.sum(lp[2:3] * lp[3:4], axis=-1, keepdims=True)) + LAMBDA_INIT)
    for h in range(N_HEADS):
        l = l_ref[h]
        acc = acc_ref[h]
        o = acc[:, 0:t] / l[:, 0:t] - lam * (acc[:, t:2 * t] / l[:, t:2 * t])
        inv = lax.rsqrt(jnp.mean(o * o, axis=0, keepdims=True) + EPS)
        y = o * inv * gain_ref[...] * (1.0 - LAMBDA_INIT)
        out_ref[0, :, h * LANES:(h + 1) * LANES] = y.T.astype(BF16)


def _attention(bounded, table, qt, k, vt, k_meta, vt_meta, bias, bmeta, lam_params, gain_col, *,
               name):
    batch, _, _, seq = qt.shape
    t = ATTN_TILE
    assert seq % (4 * t) == 0
    n_kv = seq // t
    return pl.pallas_call(
        functools.partial(_attention_kernel, n_kv=n_kv),
        grid=(batch, n_kv),
        in_specs=[
            pl.BlockSpec(memory_space=pltpu.SMEM),
            pl.BlockSpec(memory_space=pltpu.SMEM),
            pl.BlockSpec((1, N_HEADS, LANES, t), lambda b, q: (b, 0, 0, q)),
            pl.BlockSpec((1, N_HEADS, seq, LANES), lambda b, q: (b, 0, 0, 0)),
            pl.BlockSpec((1, N_HEADS, LANES, seq), lambda b, q: (b, 0, 0, 0)),
            _resident(k_meta.shape),
            _resident(vt_meta.shape),
            _resident(bias.shape),
            pl.BlockSpec((N_HEADS, 1, LANES, 2 * t), lambda b, q: (0, jnp.minimum(q, 1), 0, 0)),
            _resident((4, HEAD_DQK)),
            _resident((HEAD_DV, 1)),
        ],
        out_specs=pl.BlockSpec((1, t, ATTN_WIDTH), lambda b, q: (b, q, 0)),
        out_shape=jax.ShapeDtypeStruct((batch, seq, ATTN_WIDTH), BF16),
        scratch_shapes=[
            pltpu.VMEM((N_HEADS, LANES, 2 * t), BF16),
            pltpu.VMEM((N_HEADS, 1, 2 * t), F32),
            pltpu.VMEM((N_HEADS, 1, 2 * t), F32),
            pltpu.VMEM((N_HEADS, HEAD_DV, 2 * t), F32),
            pltpu.VMEM((N_HEADS, t, 2 * t), F32),
        ] + [pltpu.VMEM((2, 2 * t, 2 * t), BF16) for _ in range(N_HEADS)],
        compiler_params=pltpu.CompilerParams(
            dimension_semantics=("parallel", "arbitrary"),
            vmem_limit_bytes=VMEM_LIMIT_BYTES,
        ),
        name=name,
    )(bounded, table, qt, k, vt, k_meta, vt_meta, bias, bmeta, lam_params, gain_col)


def _bounded_flags(stats, k_meta_sq, v_meta_max, table, *, batch, seq):
    tiles = seq // ROW_TILE
    groups = 2 * N_HEADS
    q_sq = stats[:, 0, :groups].reshape(batch, tiles, groups)
    k_sq = stats[:, 1, :groups].reshape(batch, tiles, groups).max(axis=1)
    k_sq = jnp.maximum(k_sq, k_meta_sq[None, :])
    v_max = jnp.maximum(stats[:, 2, 0].reshape(batch, tiles).max(axis=1), v_meta_max)
    logit_bound = (jnp.sqrt(q_sq * k_sq[:, None, :]).max(axis=-1) * NORM_MARGIN
                   + jnp.max(jnp.abs(table)))
    ok = jnp.logical_and(logit_bound < BOUNDED_LOGIT_LIMIT,
                         (v_max < BOUNDED_VALUE_LIMIT)[:, None])
    return jnp.repeat(ok.astype(jnp.int32), ROW_TILE // ATTN_TILE, axis=1)


def _mix_ffn2_kernel(h1_ref, attn_ref, pool_ref, prev_ref, next_ref, mhalo_ref, pw_ref, ps_ref,
                     wout_ref, g2_ref, wg_ref, wu_ref, wd_ref, gf_ref, y_ref,
                     x_ref, sum2_ref, sum4_ref, sum8_ref, sum16_ref, pooled_ref, xn_ref, acc_ref,
                     *, tiles_per_seq):
    sum_refs = (sum2_ref, sum4_ref, sum8_ref, sum16_ref)
    assert tuple(2 << k for k in range(len(sum_refs))) == POOL_WINDOWS
    tm = h1_ref.shape[0]
    halo = POOL_HALO
    lead = 2 * halo
    ext = tm + 2 * halo
    tile = pl.program_id(0) % tiles_per_seq
    last_tile = tile == tiles_per_seq - 1
    x_ref[0:halo, :] = jnp.zeros((halo, POOL_WIDTH), F32)
    x_ref[halo:lead, :] = jnp.where(tile == 0, mhalo_ref[...], prev_ref[...])
    x_ref[lead:lead + tm, :] = pool_ref[...]
    x_ref[lead + tm:, :] = jnp.where(last_tile, 0.0, next_ref[...])

    src = x_ref
    for k, dst in enumerate(sum_refs):
        cols = slice(k * POOL_GROUP, POOL_WIDTH)
        reach = 1 << k
        dst[0:halo, :] = jnp.zeros((halo, POOL_WIDTH), F32)
        dst[halo:halo + ext, cols] = (src[halo:halo + ext, cols]
                                      + src[halo - reach:halo - reach + ext, cols])
        src = dst

    def pooled_rows(g, w, rows, cnt):
        cols = slice(g * POOL_GROUP, (g + 1) * POOL_GROUP)
        end = w // 2 - 1
        total = sum_refs[g][lead + rows.start + end:lead + rows.stop + end, cols]
        return (total / cnt - x_ref[lead + rows.start:lead + rows.stop, cols]).astype(BF16)

    for g, w in enumerate(POOL_WINDOWS):
        pooled_ref[:, g * POOL_GROUP:(g + 1) * POOL_GROUP] = pooled_rows(
            g, w, slice(0, tm), float(w))

    @pl.when(last_tile)
    def _():
        rows = slice(tm - halo, tm)
        remaining = halo - lax.broadcasted_iota(jnp.int32, (halo, 1), 0)
        for g, w in enumerate(POOL_WINDOWS):
            cnt = jnp.minimum(w, remaining + w // 2).astype(F32)
            pooled_ref[rows, g * POOL_GROUP:(g + 1) * POOL_GROUP] = pooled_rows(g, w, rows, cnt)

    xn_ref[:, 0:ATTN_WIDTH] = attn_ref[...]
    for g in range(len(POOL_WINDOWS)):
        cols = slice(g * POOL_GROUP, (g + 1) * POOL_GROUP)
        mixed = jnp.dot(pooled_ref[:, cols], pw_ref[g], preferred_element_type=F32) * ps_ref[:, cols]
        xn_ref[:, ATTN_WIDTH + g * POOL_GROUP:ATTN_WIDTH + (g + 1) * POOL_GROUP] = (
            mixed.astype(BF16))
    h2 = h1_ref[...] + jnp.dot(xn_ref[...], wout_ref[...], preferred_element_type=F32)
    xn_ref[...] = _rms(h2, g2_ref[...]).astype(BF16)
    _swiglu_into(acc_ref, xn_ref, wg_ref, wu_ref, wd_ref)
    y_ref[...] = _rms(h2 + 0.5 * acc_ref[...], gf_ref[...])


def _mix_ffn2(h1, attn, pool, meta_halo, pw, ps, wout, g2, wg, wu, wd, gf, *, seq, name):
    rows = h1.shape[0]
    tm = ROW_TILE
    halo = POOL_HALO
    assert rows % tm == 0 and seq % tm == 0 and tm % halo == 0
    halo_blocks = rows // halo
    per_tile = tm // halo
    sums = pltpu.VMEM((tm + 3 * halo, POOL_WIDTH), F32)
    return pl.pallas_call(
        functools.partial(_mix_ffn2_kernel, tiles_per_seq=seq // tm),
        grid=(rows // tm,),
        in_specs=[
            pl.BlockSpec((tm, D_MODEL), lambda i: (i, 0)),
            pl.BlockSpec((tm, ATTN_WIDTH), lambda i: (i, 0)),
            pl.BlockSpec((tm, POOL_WIDTH), lambda i: (i, 0)),
            pl.BlockSpec((halo, POOL_WIDTH), lambda i: (jnp.maximum(i * per_tile - 1, 0), 0)),
            pl.BlockSpec((halo, POOL_WIDTH),
                         lambda i: (jnp.minimum((i + 1) * per_tile, halo_blocks - 1), 0)),
            _resident((halo, POOL_WIDTH)),
            _resident(pw.shape),
            _resident((1, POOL_WIDTH)),
            _resident(wout.shape),
            _resident((1, D_MODEL)),
            _resident(wg.shape), _resident(wu.shape), _resident(wd.shape),
            _resident((1, D_MODEL)),
        ],
        out_specs=pl.BlockSpec((tm, D_MODEL), lambda i: (i, 0)),
        out_shape=jax.ShapeDtypeStruct((rows, D_MODEL), F32),
        scratch_shapes=[
            sums, sums, sums, sums, sums,
            pltpu.VMEM((tm, POOL_WIDTH), BF16),
            pltpu.VMEM((tm, D_MODEL), BF16),
            pltpu.VMEM((tm, D_MODEL), F32),
        ],
        compiler_params=pltpu.CompilerParams(
            dimension_semantics=("parallel",), vmem_limit_bytes=VMEM_LIMIT_BYTES),
        name=name,
    )(h1, attn, pool, pool, pool, meta_halo, pw, ps, wout, g2, wg, wu, wd, gf)


def _chunk_cols(w):
    return w.reshape(D_MODEL, N_FF_CHUNKS, FF_CHUNK).transpose(1, 0, 2).astype(BF16)


def _chunk_rows(w):
    return w.reshape(N_FF_CHUNKS, FF_CHUNK, D_MODEL).astype(BF16)


def kernel(x_prompt, x_sample, meta_tokens, rel_bias_table, norm_ffn1, ffn1_w_gate, ffn1_w_up,
           ffn1_w_down, norm_mix, w_in, lambda_q1, lambda_k1, lambda_q2, lambda_k2, subln_gain,
           pool_w, pool_scale, w_out, norm_ffn2, ffn2_w_gate, ffn2_w_up, ffn2_w_down, norm_final):
    layer = 0
    g1 = norm_ffn1[layer].reshape(1, D_MODEL)
    gm = norm_mix[layer].reshape(1, D_MODEL)
    g2 = norm_ffn2[layer].reshape(1, D_MODEL)
    gf = norm_final.reshape(1, D_MODEL)
    f1 = (_chunk_cols(ffn1_w_gate[layer]), _chunk_cols(ffn1_w_up[layer]),
          _chunk_rows(ffn1_w_down[layer]))
    f2 = (_chunk_cols(ffn2_w_gate[layer]), _chunk_cols(ffn2_w_up[layer]),
          _chunk_rows(ffn2_w_down[layer]))
    win = w_in[layer].astype(BF16)
    wout = w_out[layer].astype(BF16)
    pw = pool_w[layer].astype(BF16)
    ps = pool_scale[layer].reshape(1, POOL_WIDTH)
    lam_params = jnp.stack([lambda_q1[layer], lambda_k1[layer], lambda_q2[layer],
                            lambda_k2[layer]]).astype(F32)
    gain_col = subln_gain[layer].reshape(HEAD_DV, 1)
    table = rel_bias_table.astype(F32) * LOG2E

    _, u_meta = _ffn1_inproj(meta_tokens.astype(F32), g1, *f1, gm, win, seq=None,
                             name="ffn1_inproj_meta")
    pad = LANES - N_META
    k_meta = u_meta[:, QK_COLS:2 * QK_COLS].reshape(N_META, N_HEADS, LANES)
    k_meta = jnp.pad(k_meta.transpose(1, 0, 2), ((0, 0), (0, pad), (0, 0))).astype(BF16)
    v_meta = u_meta[:, 2 * QK_COLS:2 * QK_COLS + ATTN_WIDTH].reshape(N_META, N_HEADS, HEAD_DV)
    vt_meta = jnp.pad(v_meta.transpose(1, 2, 0), ((0, 0), (0, 0), (0, pad))).astype(BF16)
    meta_halo = u_meta[N_META - POOL_HALO:, 2 * QK_COLS + ATTN_WIDTH:]
    k_meta_sq = jnp.square(u_meta[:, QK_COLS:2 * QK_COLS]).reshape(
        N_META, 2 * N_HEADS, HEAD_DQK).sum(axis=-1).max(axis=0)
    v_meta_max = jnp.max(jnp.abs(v_meta))

    bias, bmeta = _bias_tiles(table)

    def encode(x, tag):
        batch, seq, _ = x.shape
        rows = batch * seq
        h1, qt, k, vt, pool, stats = _ffn1_inproj(x.reshape(rows, D_MODEL), g1, *f1, gm, win,
                                                  seq=seq, name="ffn1_inproj_" + tag)
        bounded = _bounded_flags(stats, k_meta_sq, v_meta_max, table, batch=batch, seq=seq)
        attn = _attention(bounded, table, qt, k, vt, k_meta, vt_meta, bias, bmeta, lam_params, gain_col,
                          name="attention_" + tag)
        y = _mix_ffn2(h1, attn.reshape(rows, ATTN_WIDTH), pool, meta_halo, pw, ps, wout, g2, *f2,
                      gf, seq=seq, name="mix_ffn2_" + tag)
        return y.reshape(batch, seq, D_MODEL)

    return (encode(x_prompt, "prompt"), encode(x_sample, "sample"))
```

```python
import functools
import math

import numpy as np
import jax
import jax.numpy as jnp
from jax import lax
from jax.experimental import pallas as pl
from jax.experimental.pallas import tpu as pltpu

D_MODEL = 1024
N_META = 16
D_FF = 2816
N_HEADS = 4
HEAD_DV = 128
HEAD_DQK = 64
QK_COLS = N_HEADS * 2 * HEAD_DQK
ATTN_WIDTH = N_HEADS * HEAD_DV
POOL_WINDOWS = (2, 4, 8, 16)
POOL_GROUP = 128
POOL_WIDTH = len(POOL_WINDOWS) * POOL_GROUP
IN_COLS = 2 * QK_COLS + ATTN_WIDTH + POOL_WIDTH
N_BUCKETS = 32
MAX_DISTANCE = 128
EPS = 1e-6
LAMBDA_INIT = 0.8 - 0.6 * math.exp(-0.3 * 0)
LOG2E = math.log2(math.e)
QK_SCALE_LOG2 = HEAD_DQK ** -0.5 * LOG2E

LANES = 128
F32_SUBLANES = 8
MXU_DIM = 256
VMEM_LIMIT_BYTES = 56 * 1024 * 1024

ROW_TILE = 512
FF_CHUNK = MXU_DIM
N_FF_CHUNKS = D_FF // FF_CHUNK
ATTN_TILE = 256
QUERY_TILE = 512
POOL_HALO = max(POOL_WINDOWS) // 2
MASK_BIAS = -1e30
BIAS_SATURATION_DISTANCE = 91
KEY_PAIR_UNROLL = 4
N_BIAS_TILES = QUERY_TILE // ATTN_TILE + 4

BOUNDED_LOGIT_LIMIT = 48.0
BOUNDED_VALUE_LIMIT = 1e15
NORM_MARGIN = 1.05

assert D_FF % FF_CHUNK == 0
assert POOL_HALO == F32_SUBLANES and N_META >= POOL_HALO
assert ATTN_TILE >= BIAS_SATURATION_DISTANCE
assert QUERY_TILE % ATTN_TILE == 0 and ROW_TILE % QUERY_TILE == 0

F32 = jnp.float32
BF16 = jnp.bfloat16


def _rel_bucket_np(rel):
    half = N_BUCKETS // 2
    max_exact = half // 2
    ret = np.where(rel > 0, half, 0)
    n = np.abs(rel)
    nf = np.maximum(n, 1).astype(np.float64)
    large = max_exact + (np.log(nf / max_exact) / math.log(MAX_DISTANCE / max_exact)
                         * (half - max_exact)).astype(np.int32)
    large = np.minimum(large, half - 1)
    return (ret + np.where(n < max_exact, n, large)).astype(np.int32)


def _check_saturation():
    n = np.arange(BIAS_SATURATION_DISTANCE, 1 << 15)
    assert np.all(_rel_bucket_np(-n) == N_BUCKETS // 2 - 1)
    assert np.all(_rel_bucket_np(n) == N_BUCKETS - 1)


_check_saturation()


def _rms(x, gain):
    return x * lax.rsqrt(jnp.mean(x * x, axis=-1, keepdims=True) + EPS) * gain


def _swiglu_into(acc_ref, xn_ref, wg_ref, wu_ref, wd_ref):
    acc_ref[...] = jnp.zeros_like(acc_ref)
    for j in range(N_FF_CHUNKS):
        xn = xn_ref[...]
        g = jnp.dot(xn, wg_ref[j], preferred_element_type=F32)
        u = jnp.dot(xn, wu_ref[j], preferred_element_type=F32)
        a = (g * (1.0 / (1.0 + jnp.exp(-g))) * u).astype(BF16)
        acc_ref[...] += jnp.dot(a, wd_ref[j], preferred_element_type=F32)


def _ffn1_inproj_kernel(x_ref, g1_ref, wg_ref, wu_ref, wd_ref, gm_ref, win_ref, sel_ref, *refs,
                        transposed_heads):
    if transposed_heads:
        h1_ref, qt_ref, k_ref, vt_ref, pool_ref, stat_ref, xn_ref, acc_ref = refs
    else:
        h1_ref, u_ref, xn_ref, acc_ref = refs
    x = x_ref[...]
    xn_ref[...] = _rms(x, g1_ref[...]).astype(BF16)
    _swiglu_into(acc_ref, xn_ref, wg_ref, wu_ref, wd_ref)
    h1 = x + 0.5 * acc_ref[...]
    h1_ref[...] = h1
    hn = _rms(h1, gm_ref[...]).astype(BF16)
    if not transposed_heads:
        u_ref[...] = jnp.dot(hn, win_ref[...], preferred_element_type=F32)
        return
    uq = jnp.dot(hn, win_ref[:, 0:QK_COLS], preferred_element_type=F32) * QK_SCALE_LOG2
    for h in range(N_HEADS):
        qt_ref[0, h] = uq[:, h * LANES:(h + 1) * LANES].T.astype(BF16)
    uk = jnp.dot(hn, win_ref[:, QK_COLS:2 * QK_COLS], preferred_element_type=F32)
    for h in range(N_HEADS):
        k_ref[0, h] = uk[:, h * LANES:(h + 1) * LANES].astype(BF16)
    uv = jnp.dot(hn, win_ref[:, 2 * QK_COLS:2 * QK_COLS + ATTN_WIDTH],
                 preferred_element_type=F32)
    for h in range(N_HEADS):
        vt_ref[0, h] = uv[:, h * LANES:(h + 1) * LANES].T.astype(BF16)
    pool_ref[...] = jnp.dot(hn, win_ref[:, 2 * QK_COLS + ATTN_WIDTH:],
                            preferred_element_type=F32)

    def max_sq_norms(u):
        sq = jnp.dot((u * u).astype(BF16), sel_ref[...], preferred_element_type=F32)
        return jnp.max(sq, axis=0, keepdims=True)

    stat_ref[0, 0:1, :] = max_sq_norms(uq)
    stat_ref[0, 1:2, :] = max_sq_norms(uk)
    stat_ref[0, 2:3, :] = jnp.broadcast_to(jnp.max(jnp.abs(uv), keepdims=True), (1, LANES))
    stat_ref[0, 3:, :] = jnp.zeros((F32_SUBLANES - 3, LANES), F32)


def _resident(shape):
    zeros = (0,) * len(shape)
    return pl.BlockSpec(shape, lambda *_: zeros, pipeline_mode=pl.Buffered(1))


def _ffn1_inproj(x, g1, wg, wu, wd, gm, win, *, seq, name):
    rows = x.shape[0]
    transposed_heads = seq is not None
    tm = ROW_TILE if transposed_heads else rows
    assert rows % tm == 0
    in_specs = [
        pl.BlockSpec((tm, D_MODEL), lambda i: (i, 0)),
        _resident((1, D_MODEL)),
        _resident(wg.shape), _resident(wu.shape), _resident(wd.shape),
        _resident((1, D_MODEL)),
        _resident(win.shape),
        _resident((QK_COLS, LANES)),
    ]
    sel = (np.arange(QK_COLS)[:, None] // HEAD_DQK == np.arange(LANES)[None, :])
    row_spec = pl.BlockSpec((tm, D_MODEL), lambda i: (i, 0))
    if transposed_heads:
        assert seq % tm == 0
        tpb = seq // tm
        batch = rows // seq
        head_t = pl.BlockSpec((1, N_HEADS, LANES, tm), lambda i: (i // tpb, 0, 0, i % tpb))
        head_n = pl.BlockSpec((1, N_HEADS, tm, LANES), lambda i: (i // tpb, 0, i % tpb, 0))
        out_specs = [row_spec, head_t, head_n, head_t,
                     pl.BlockSpec((tm, POOL_WIDTH), lambda i: (i, 0)),
                     pl.BlockSpec((1, F32_SUBLANES, LANES), lambda i: (i, 0, 0))]
        out_shape = [
            jax.ShapeDtypeStruct((rows, D_MODEL), F32),
            jax.ShapeDtypeStruct((batch, N_HEADS, LANES, seq), BF16),
            jax.ShapeDtypeStruct((batch, N_HEADS, seq, LANES), BF16),
            jax.ShapeDtypeStruct((batch, N_HEADS, LANES, seq), BF16),
            jax.ShapeDtypeStruct((rows, POOL_WIDTH), F32),
            jax.ShapeDtypeStruct((rows // tm, F32_SUBLANES, LANES), F32),
        ]
    else:
        out_specs = [row_spec, pl.BlockSpec((tm, IN_COLS), lambda i: (i, 0))]
        out_shape = [jax.ShapeDtypeStruct((rows, D_MODEL), F32),
                     jax.ShapeDtypeStruct((rows, IN_COLS), F32)]
    return pl.pallas_call(
        functools.partial(_ffn1_inproj_kernel, transposed_heads=transposed_heads),
        grid=(rows // tm,),
        in_specs=in_specs,
        out_specs=out_specs,
        out_shape=out_shape,
        scratch_shapes=[pltpu.VMEM((tm, D_MODEL), BF16), pltpu.VMEM((tm, D_MODEL), F32)],
        compiler_params=pltpu.CompilerParams(
            dimension_semantics=("parallel",), vmem_limit_bytes=VMEM_LIMIT_BYTES),
        name=name,
    )(x, g1, wg, wu, wd, gm, win, jnp.asarray(sel, BF16))


def _bias_tiles_kernel(table_ref, bdiag_ref, bmeta_ref, diag_ref, meta_ref):
    h = pl.program_id(0)
    t, tq = ATTN_TILE, QUERY_TILE

    def lookup(bucket):
        out = jnp.zeros(bucket.shape, F32)
        for b in range(N_BUCKETS):
            out = jnp.where(bucket == b, table_ref[b, h], out)
        return out

    diag_ref[0, 0] = jnp.full((t, tq), table_ref[N_BUCKETS // 2 - 1, h], F32)
    diag_ref[0, N_BIAS_TILES - 1] = jnp.full((t, tq), table_ref[N_BUCKETS - 1, h], F32)
    for d in range(N_BIAS_TILES - 2):
        diag_ref[0, d + 1] = lookup(bdiag_ref[d])
    bm = bmeta_ref[...]
    meta_ref[0, 0] = jnp.where(bm < 0, MASK_BIAS, lookup(bm))
    meta_ref[0, 1] = jnp.where(bm < 0, MASK_BIAS, table_ref[N_BUCKETS // 2 - 1, h])


def _bias_tiles(table):
    t, tq = ATTN_TILE, QUERY_TILE
    near = N_BIAS_TILES - 2
    r = np.arange(t)[:, None]
    c = np.arange(tq)[None, :]
    bdiag = np.stack([_rel_bucket_np((d - 1) * t + r - c) for d in range(near)])
    rm = np.arange(LANES)[:, None]
    bmeta = np.where(rm < N_META, _rel_bucket_np(rm - N_META - c), -1).astype(np.int32)
    return pl.pallas_call(
        _bias_tiles_kernel,
        grid=(N_HEADS,),
        in_specs=[
            pl.BlockSpec(memory_space=pltpu.SMEM),
            pl.BlockSpec((near, t, tq), lambda h: (0, 0, 0)),
            pl.BlockSpec((LANES, tq), lambda h: (0, 0)),
        ],
        out_specs=[
            pl.BlockSpec((1, N_BIAS_TILES, t, tq), lambda h: (h, 0, 0, 0)),
            pl.BlockSpec((1, 2, LANES, tq), lambda h: (h, 0, 0, 0)),
        ],
        out_shape=[
            jax.ShapeDtypeStruct((N_HEADS, N_BIAS_TILES, t, tq), F32),
            jax.ShapeDtypeStruct((N_HEADS, 2, LANES, tq), F32),
        ],
        name="bias_tiles",
    )(table, jnp.asarray(bdiag), jnp.asarray(bmeta))


def _attention_kernel(flag_ref, qt_ref, k_ref, vt_ref, km_ref, vtm_ref, bias_ref, bmeta_ref,
                      lam_ref, gain_ref, out_ref, rhs_ref, m_ref, l_ref, acc_ref, s_ref, *p_refs,
                      n_kv):
    t, tq = ATTN_TILE, QUERY_TILE
    maps = (slice(0, tq), slice(tq, 2 * tq))
    qi = pl.program_id(1)

    for h in range(N_HEADS):
        qt = qt_ref[0, h]
        upper = lax.broadcasted_iota(jnp.int32, qt.shape, 0) < HEAD_DQK
        zero = jnp.zeros_like(qt)
        rhs_ref[h, :, maps[0]] = jnp.where(upper, qt, zero)
        rhs_ref[h, :, maps[1]] = jnp.where(upper, zero, qt)

    m_ref[...] = jnp.full(m_ref.shape, MASK_BIAS, F32)
    l_ref[...] = jnp.zeros_like(l_ref)
    acc_ref[...] = jnp.zeros_like(acc_ref)

    def key_block(j):
        start = pl.multiple_of(j * t, t)
        tile = jnp.clip(j - qi * (tq // t) + 2, 0, N_BIAS_TILES - 1)
        return (lambda h: k_ref[0, h, pl.ds(start, t), :],
                lambda h: vt_ref[0, h, :, pl.ds(start, t)],
                lambda h: bias_ref[h, tile])

    def meta_block():
        return (lambda h: km_ref[h], lambda h: vtm_ref[h], lambda h: bmeta_ref[h, 0])

    def logits(h, kc, bias):
        s = jnp.dot(kc, rhs_ref[h], preferred_element_type=F32)
        return [s[:, cols] + bias for cols in maps]

    def bounded_blocks():
        def weigh(h, slot, half, kc, bias):
            rows = slice(half * t, half * t + kc.shape[0])
            for cols, s in zip(maps, logits(h, kc, bias)):
                p = jnp.exp2(s)
                p_refs[h][slot, rows, cols] = p.astype(BF16)
                l_ref[h, :, cols] += jnp.sum(p, axis=0, keepdims=True)

        def accumulate(h, slot, vtc):
            acc_ref[h] += jnp.dot(vtc, p_refs[h][slot, 0:vtc.shape[1]],
                                  preferred_element_type=F32)

        def pair_values(i):
            start = pl.multiple_of(i * 2 * t, 2 * t)
            return lambda h: vt_ref[0, h, :, pl.ds(start, 2 * t)]

        def step(slot, i, prev):
            first, second = key_block(2 * i), key_block(2 * i + 1)
            for h in range(N_HEADS):
                weigh(h, slot, 0, first[0](h), first[2](h))
                if prev is not None:
                    accumulate(h, 1 - slot, pair_values(prev)(h))
                weigh(h, slot, 1, second[0](h), second[2](h))

        n_pairs = n_kv // 2
        step(0, 0, None)

        def body(i, carry):
            j = 2 * i + 1
            step(1, j, j - 1)
            step(0, j + 1, j)
            return carry

        lax.fori_loop(0, n_pairs // 2 - 1, body, 0, unroll=KEY_PAIR_UNROLL)
        step(1, n_pairs - 1, n_pairs - 2)
        get_k, get_vt, get_bias = meta_block()
        for h in range(N_HEADS):
            weigh(h, 0, 0, get_k(h), get_bias(h))
            accumulate(h, 1, pair_values(n_pairs - 1)(h))
        for h in range(N_HEADS):
            accumulate(h, 0, get_vt(h))

    def general_blocks():
        def step(get_k, get_vt, get_bias):
            alphas = []
            for h in range(N_HEADS):
                kc = get_k(h)
                rows = kc.shape[0]
                for cols, s in zip(maps, logits(h, kc, get_bias(h))):
                    s_ref[h, 0:rows, cols] = s
                m_old = m_ref[h]
                m_new = jnp.maximum(m_old, jnp.max(s_ref[h, 0:rows], axis=0, keepdims=True))
                alphas.append(jnp.exp2(m_old - m_new))
                p = jnp.exp2(s_ref[h, 0:rows] - m_new)
                p_refs[h][0, 0:rows] = p.astype(BF16)
                l_ref[h] = alphas[h] * l_ref[h] + jnp.sum(p, axis=0, keepdims=True)
                m_ref[h] = m_new
            for h in range(N_HEADS):
                vtc = get_vt(h)
                acc_ref[h] = alphas[h] * acc_ref[h] + jnp.dot(
                    vtc, p_refs[h][0, 0:vtc.shape[1]], preferred_element_type=F32)

        step(*meta_block())

        def body(j, carry):
            step(*key_block(j))
            return carry

        lax.fori_loop(0, n_kv, body, 0)

    bounded = flag_ref[pl.program_id(0), qi] != 0
    pl.when(bounded)(bounded_blocks)
    pl.when(jnp.logical_not(bounded))(general_blocks)

    lp = lam_ref[...]
    lam = (jnp.exp(jnp.sum(lp[0:1] * lp[1:2], axis=-1, keepdims=True))
           - jnp.exp(jnp.sum(lp[2:3] * lp[3:4], axis=-1, keepdims=True)) + LAMBDA_INIT)
    for h in range(N_HEADS):
        l = l_ref[h]
        acc = acc_ref[h]
        o = acc[:, maps[0]] / l[:, maps[0]] - lam * (acc[:, maps[1]] / l[:, maps[1]])
        inv = lax.rsqrt(jnp.mean(o * o, axis=0, keepdims=True) + EPS)
        y = o * inv * gain_ref[...] * (1.0 - LAMBDA_INIT)
        out_ref[0, :, h * LANES:(h + 1) * LANES] = y.T.astype(BF16)


def _attention(bounded, qt, k, vt, k_meta, vt_meta, bias, bmeta, lam_params, gain_col, *, name):
    batch, _, _, seq = qt.shape
    t, tq = ATTN_TILE, QUERY_TILE
    assert seq % (4 * t) == 0 and seq % tq == 0
    n_kv = seq // t
    return pl.pallas_call(
        functools.partial(_attention_kernel, n_kv=n_kv),
        grid=(batch, seq // tq),
        in_specs=[
            pl.BlockSpec(memory_space=pltpu.SMEM),
            pl.BlockSpec((1, N_HEADS, LANES, tq), lambda b, q: (b, 0, 0, q)),
            pl.BlockSpec((1, N_HEADS, seq, LANES), lambda b, q: (b, 0, 0, 0),
                         pipeline_mode=pl.Buffered(1)),
            pl.BlockSpec((1, N_HEADS, LANES, seq), lambda b, q: (b, 0, 0, 0),
                         pipeline_mode=pl.Buffered(1)),
            _resident(k_meta.shape),
            _resident(vt_meta.shape),
            _resident(bias.shape),
            pl.BlockSpec((N_HEADS, 1, LANES, tq), lambda b, q: (0, jnp.minimum(q, 1), 0, 0)),
            _resident((4, HEAD_DQK)),
            _resident((HEAD_DV, 1)),
        ],
        out_specs=pl.BlockSpec((1, tq, ATTN_WIDTH), lambda b, q: (b, q, 0)),
        out_shape=jax.ShapeDtypeStruct((batch, seq, ATTN_WIDTH), BF16),
        scratch_shapes=[
            pltpu.VMEM((N_HEADS, LANES, 2 * tq), BF16),
            pltpu.VMEM((N_HEADS, 1, 2 * tq), F32),
            pltpu.VMEM((N_HEADS, 1, 2 * tq), F32),
            pltpu.VMEM((N_HEADS, HEAD_DV, 2 * tq), F32),
            pltpu.VMEM((N_HEADS, t, 2 * tq), F32),
        ] + [pltpu.VMEM((2, 2 * t, 2 * tq), BF16) for _ in range(N_HEADS)],
        compiler_params=pltpu.CompilerParams(
            dimension_semantics=("parallel", "arbitrary"),
            vmem_limit_bytes=VMEM_LIMIT_BYTES,
        ),
        name=name,
    )(bounded, qt, k, vt, k_meta, vt_meta, bias, bmeta, lam_params, gain_col)


def _bounded_flags(stats, k_meta_sq, v_meta_max, table, *, batch, seq):
    tiles = seq // ROW_TILE
    groups = 2 * N_HEADS
    q_sq = stats[:, 0, :groups].reshape(batch, tiles, groups)
    k_sq = stats[:, 1, :groups].reshape(batch, tiles, groups).max(axis=1)
    k_sq = jnp.maximum(k_sq, k_meta_sq[None, :])
    v_max = jnp.maximum(stats[:, 2, 0].reshape(batch, tiles).max(axis=1), v_meta_max)
    logit_bound = (jnp.sqrt(q_sq * k_sq[:, None, :]).max(axis=-1) * NORM_MARGIN
                   + jnp.max(jnp.abs(table)))
    ok = jnp.logical_and(logit_bound < BOUNDED_LOGIT_LIMIT,
                         (v_max < BOUNDED_VALUE_LIMIT)[:, None])
    return jnp.repeat(ok.astype(jnp.int32), ROW_TILE // QUERY_TILE, axis=1)


def _mix_ffn2_kernel(h1_ref, attn_ref, pool_ref, prev_ref, next_ref, mhalo_ref, pw_ref, ps_ref,
                     wout_ref, g2_ref, wg_ref, wu_ref, wd_ref, gf_ref, y_ref,
                     x_ref, sum2_ref, sum4_ref, sum8_ref, sum16_ref, pooled_ref, xn_ref, acc_ref,
                     *, tiles_per_seq):
    sum_refs = (sum2_ref, sum4_ref, sum8_ref, sum16_ref)
    assert tuple(2 << k for k in range(len(sum_refs))) == POOL_WINDOWS
    tm = h1_ref.shape[0]
    halo = POOL_HALO
    lead = 2 * halo
    ext = tm + 2 * halo
    tile = pl.program_id(0) % tiles_per_seq
    last_tile = tile == tiles_per_seq - 1
    x_ref[0:halo, :] = jnp.zeros((halo, POOL_WIDTH), F32)
    x_ref[halo:lead, :] = jnp.where(tile == 0, mhalo_ref[...], prev_ref[...])
    x_ref[lead:lead + tm, :] = pool_ref[...]
    x_ref[lead + tm:, :] = jnp.where(last_tile, 0.0, next_ref[...])

    src = x_ref
    for k, dst in enumerate(sum_refs):
        cols = slice(k * POOL_GROUP, POOL_WIDTH)
        reach = 1 << k
        dst[0:halo, :] = jnp.zeros((halo, POOL_WIDTH), F32)
        dst[halo:halo + ext, cols] = (src[halo:halo + ext, cols]
                                      + src[halo - reach:halo - reach + ext, cols])
        src = dst

    def pooled_rows(g, w, rows, cnt):
        cols = slice(g * POOL_GROUP, (g + 1) * POOL_GROUP)
        end = w // 2 - 1
        total = sum_refs[g][lead + rows.start + end:lead + rows.stop + end, cols]
        return (total / cnt - x_ref[lead + rows.start:lead + rows.stop, cols]).astype(BF16)

    for g, w in enumerate(POOL_WINDOWS):
        pooled_ref[:, g * POOL_GROUP:(g + 1) * POOL_GROUP] = pooled_rows(
            g, w, slice(0, tm), float(w))

    @pl.when(last_tile)
    def _():
        rows = slice(tm - halo, tm)
        remaining = halo - lax.broadcasted_iota(jnp.int32, (halo, 1), 0)
        for g, w in enumerate(POOL_WINDOWS):
            cnt = jnp.minimum(w, remaining + w // 2).astype(F32)
            pooled_ref[rows, g * POOL_GROUP:(g + 1) * POOL_GROUP] = pooled_rows(g, w, rows, cnt)

    xn_ref[:, 0:ATTN_WIDTH] = attn_ref[...]
    for g in range(len(POOL_WINDOWS)):
        cols = slice(g * POOL_GROUP, (g + 1) * POOL_GROUP)
        mixed = jnp.dot(pooled_ref[:, cols], pw_ref[g], preferred_element_type=F32) * ps_ref[:, cols]
        xn_ref[:, ATTN_WIDTH + g * POOL_GROUP:ATTN_WIDTH + (g + 1) * POOL_GROUP] = (
            mixed.astype(BF16))
    h2 = h1_ref[...] + jnp.dot(xn_ref[...], wout_ref[...], preferred_element_type=F32)
    xn_ref[...] = _rms(h2, g2_ref[...]).astype(BF16)
    _swiglu_into(acc_ref, xn_ref, wg_ref, wu_ref, wd_ref)
    y_ref[...] = _rms(h2 + 0.5 * acc_ref[...], gf_ref[...])


def _mix_ffn2(h1, attn, pool, meta_halo, pw, ps, wout, g2, wg, wu, wd, gf, *, seq, name):
    rows = h1.shape[0]
    tm = ROW_TILE
    halo = POOL_HALO
    assert rows % tm == 0 and seq % tm == 0 and tm % halo == 0
    halo_blocks = rows // halo
    per_tile = tm // halo
    sums = pltpu.VMEM((tm + 3 * halo, POOL_WIDTH), F32)
    return pl.pallas_call(
        functools.partial(_mix_ffn2_kernel, tiles_per_seq=seq // tm),
        grid=(rows // tm,),
        in_specs=[
            pl.BlockSpec((tm, D_MODEL), lambda i: (i, 0)),
            pl.BlockSpec((tm, ATTN_WIDTH), lambda i: (i, 0)),
            pl.BlockSpec((tm, POOL_WIDTH), lambda i: (i, 0)),
            pl.BlockSpec((halo, POOL_WIDTH), lambda i: (jnp.maximum(i * per_tile - 1, 0), 0)),
            pl.BlockSpec((halo, POOL_WIDTH),
                         lambda i: (jnp.minimum((i + 1) * per_tile, halo_blocks - 1), 0)),
            _resident((halo, POOL_WIDTH)),
            _resident(pw.shape),
            _resident((1, POOL_WIDTH)),
            _resident(wout.shape),
            _resident((1, D_MODEL)),
            _resident(wg.shape), _resident(wu.shape), _resident(wd.shape),
            _resident((1, D_MODEL)),
        ],
        out_specs=pl.BlockSpec((tm, D_MODEL), lambda i: (i, 0)),
        out_shape=jax.ShapeDtypeStruct((rows, D_MODEL), F32),
        scratch_shapes=[
            sums, sums, sums, sums, sums,
            pltpu.VMEM((tm, POOL_WIDTH), BF16),
            pltpu.VMEM((tm, D_MODEL), BF16),
            pltpu.VMEM((tm, D_MODEL), F32),
        ],
        compiler_params=pltpu.CompilerParams(
            dimension_semantics=("parallel",), vmem_limit_bytes=VMEM_LIMIT_BYTES),
        name=name,
    )(h1, attn, pool, pool, pool, meta_halo, pw, ps, wout, g2, wg, wu, wd, gf)


def _chunk_cols(w):
    return w.reshape(D_MODEL, N_FF_CHUNKS, FF_CHUNK).transpose(1, 0, 2).astype(BF16)


def _chunk_rows(w):
    return w.reshape(N_FF_CHUNKS, FF_CHUNK, D_MODEL).astype(BF16)


def kernel(x_prompt, x_sample, meta_tokens, rel_bias_table, norm_ffn1, ffn1_w_gate, ffn1_w_up,
           ffn1_w_down, norm_mix, w_in, lambda_q1, lambda_k1, lambda_q2, lambda_k2, subln_gain,
           pool_w, pool_scale, w_out, norm_ffn2, ffn2_w_gate, ffn2_w_up, ffn2_w_down, norm_final):
    layer = 0
    g1 = norm_ffn1[layer].reshape(1, D_MODEL)
    gm = norm_mix[layer].reshape(1, D_MODEL)
    g2 = norm_ffn2[layer].reshape(1, D_MODEL)
    gf = norm_final.reshape(1, D_MODEL)
    f1 = (_chunk_cols(ffn1_w_gate[layer]), _chunk_cols(ffn1_w_up[layer]),
          _chunk_rows(ffn1_w_down[layer]))
    f2 = (_chunk_cols(ffn2_w_gate[layer]), _chunk_cols(ffn2_w_up[layer]),
          _chunk_rows(ffn2_w_down[layer]))
    win = w_in[layer].astype(BF16)
    wout = w_out[layer].astype(BF16)
    pw = pool_w[layer].astype(BF16)
    ps = pool_scale[layer].reshape(1, POOL_WIDTH)
    lam_params = jnp.stack([lambda_q1[layer], lambda_k1[layer], lambda_q2[layer],
                            lambda_k2[layer]]).astype(F32)
    gain_col = subln_gain[layer].reshape(HEAD_DV, 1)
    table = rel_bias_table.astype(F32) * LOG2E

    _, u_meta = _ffn1_inproj(meta_tokens.astype(F32), g1, *f1, gm, win, seq=None,
                             name="ffn1_inproj_meta")
    pad = LANES - N_META
    k_meta = u_meta[:, QK_COLS:2 * QK_COLS].reshape(N_META, N_HEADS, LANES)
    k_meta = jnp.pad(k_meta.transpose(1, 0, 2), ((0, 0), (0, pad), (0, 0))).astype(BF16)
    v_meta = u_meta[:, 2 * QK_COLS:2 * QK_COLS + ATTN_WIDTH].reshape(N_META, N_HEADS, HEAD_DV)
    vt_meta = jnp.pad(v_meta.transpose(1, 2, 0), ((0, 0), (0, 0), (0, pad))).astype(BF16)
    meta_halo = u_meta[N_META - POOL_HALO:, 2 * QK_COLS + ATTN_WIDTH:]
    k_meta_sq = jnp.square(u_meta[:, QK_COLS:2 * QK_COLS]).reshape(
        N_META, 2 * N_HEADS, HEAD_DQK).sum(axis=-1).max(axis=0)
    v_meta_max = jnp.max(jnp.abs(v_meta))

    bias, bmeta = _bias_tiles(table)

    def encode(x, tag):
        batch, seq, _ = x.shape
        rows = batch * seq
        h1, qt, k, vt, pool, stats = _ffn1_inproj(x.reshape(rows, D_MODEL), g1, *f1, gm, win,
                                                  seq=seq, name="ffn1_inproj_" + tag)
        bounded = _bounded_flags(stats, k_meta_sq, v_meta_max, table, batch=batch, seq=seq)
        attn = _attention(bounded, qt, k, vt, k_meta, vt_meta, bias, bmeta, lam_params, gain_col,
                          name="attention_" + tag)
        y = _mix_ffn2(h1, attn.reshape(rows, ATTN_WIDTH), pool, meta_halo, pw, ps, wout, g2, *f2,
                      gf, seq=seq, name="mix_ffn2_" + tag)
        return y.reshape(batch, seq, D_MODEL)

    return (encode(x_prompt, "prompt"), encode(x_sample, "sample"))
```

```python
import functools
import math

import numpy as np
import jax
import jax.numpy as jnp
from jax import lax
from jax.experimental import pallas as pl
from jax.experimental.pallas import tpu as pltpu

D_MODEL = 1024
N_META = 16
D_FF = 2816
N_HEADS = 4
HEAD_DV = 128
HEAD_DQK = 64
QK_COLS = N_HEADS * 2 * HEAD_DQK
ATTN_WIDTH = N_HEADS * HEAD_DV
POOL_WINDOWS = (2, 4, 8, 16)
POOL_GROUP = 128
POOL_WIDTH = len(POOL_WINDOWS) * POOL_GROUP
IN_COLS = 2 * QK_COLS + ATTN_WIDTH + POOL_WIDTH
N_BUCKETS = 32
MAX_DISTANCE = 128
EPS = 1e-6
LAMBDA_INIT = 0.8 - 0.6 * math.exp(-0.3 * 0)
LOG2E = math.log2(math.e)
QK_SCALE_LOG2 = HEAD_DQK ** -0.5 * LOG2E

LANES = 128
F32_SUBLANES = 8
MXU_DIM = 256
VMEM_LIMIT_BYTES = 56 * 1024 * 1024

ROW_TILE = 512
FF_CHUNK = MXU_DIM
N_FF_CHUNKS = D_FF // FF_CHUNK
ATTN_TILE = 256
POOL_HALO = max(POOL_WINDOWS) // 2
MASK_BIAS = -1e30
BIAS_SATURATION_DISTANCE = 91
KEY_PAIR_UNROLL = 4
N_BIAS_TILES = 5

BOUNDED_LOGIT_LIMIT = 48.0
BOUNDED_VALUE_LIMIT = 1e15
NORM_MARGIN = 1.05

assert D_FF % FF_CHUNK == 0
assert POOL_HALO == F32_SUBLANES and N_META >= POOL_HALO
assert ATTN_TILE >= BIAS_SATURATION_DISTANCE

F32 = jnp.float32
BF16 = jnp.bfloat16


def _rel_bucket_np(rel):
    half = N_BUCKETS // 2
    max_exact = half // 2
    ret = np.where(rel > 0, half, 0)
    n = np.abs(rel)
    nf = np.maximum(n, 1).astype(np.float64)
    large = max_exact + (np.log(nf / max_exact) / math.log(MAX_DISTANCE / max_exact)
                         * (half - max_exact)).astype(np.int32)
    large = np.minimum(large, half - 1)
    return (ret + np.where(n < max_exact, n, large)).astype(np.int32)


def _check_saturation():
    n = np.arange(BIAS_SATURATION_DISTANCE, 1 << 15)
    assert np.all(_rel_bucket_np(-n) == N_BUCKETS // 2 - 1)
    assert np.all(_rel_bucket_np(n) == N_BUCKETS - 1)


_check_saturation()


def _rms(x, gain):
    return x * lax.rsqrt(jnp.mean(x * x, axis=-1, keepdims=True) + EPS) * gain


def _swiglu_into(acc_ref, xn_ref, wg_ref, wu_ref, wd_ref):
    acc_ref[...] = jnp.zeros_like(acc_ref)
    for j in range(N_FF_CHUNKS):
        xn = xn_ref[...]
        cols = slice(j * FF_CHUNK, (j + 1) * FF_CHUNK)
        g = jnp.dot(xn, wg_ref[:, cols], preferred_element_type=F32)
        u = jnp.dot(xn, wu_ref[:, cols], preferred_element_type=F32)
        a = (g * (1.0 / (1.0 + jnp.exp(-g))) * u).astype(BF16)
        acc_ref[...] += jnp.dot(a, wd_ref[j], preferred_element_type=F32)


def _ffn1_inproj_kernel(x_ref, g1_ref, wg_ref, wu_ref, wd_ref, gm_ref, win_ref, sel_ref, *refs,
                        transposed_heads):
    if transposed_heads:
        h1_ref, qt_ref, k_ref, vt_ref, pool_ref, stat_ref, xn_ref, acc_ref = refs
    else:
        h1_ref, u_ref, xn_ref, acc_ref = refs
    x = x_ref[...]
    xn_ref[...] = _rms(x, g1_ref[...]).astype(BF16)
    _swiglu_into(acc_ref, xn_ref, wg_ref, wu_ref, wd_ref)
    h1 = x + 0.5 * acc_ref[...]
    h1_ref[...] = h1
    hn = _rms(h1, gm_ref[...]).astype(BF16)
    if not transposed_heads:
        u_ref[...] = jnp.dot(hn, win_ref[...], preferred_element_type=F32)
        return
    uq = jnp.dot(hn, win_ref[:, 0:QK_COLS], preferred_element_type=F32) * QK_SCALE_LOG2
    for h in range(N_HEADS):
        qt_ref[0, h] = uq[:, h * LANES:(h + 1) * LANES].T.astype(BF16)
    uk = jnp.dot(hn, win_ref[:, QK_COLS:2 * QK_COLS], preferred_element_type=F32)
    for h in range(N_HEADS):
        k_ref[0, h] = uk[:, h * LANES:(h + 1) * LANES].astype(BF16)
    uv = jnp.dot(hn, win_ref[:, 2 * QK_COLS:2 * QK_COLS + ATTN_WIDTH],
                 preferred_element_type=F32)
    for h in range(N_HEADS):
        vt_ref[0, h] = uv[:, h * LANES:(h + 1) * LANES].T.astype(BF16)
    pool_ref[...] = jnp.dot(hn, win_ref[:, 2 * QK_COLS + ATTN_WIDTH:],
                            preferred_element_type=F32)

    def max_sq_norms(u):
        sq = jnp.dot((u * u).astype(BF16), sel_ref[...], preferred_element_type=F32)
        return jnp.max(sq, axis=0, keepdims=True)

    stat_ref[0, 0:1, :] = max_sq_norms(uq)
    stat_ref[0, 1:2, :] = max_sq_norms(uk)
    stat_ref[0, 2:3, :] = jnp.broadcast_to(jnp.max(jnp.abs(uv), keepdims=True), (1, LANES))
    stat_ref[0, 3:, :] = jnp.zeros((F32_SUBLANES - 3, LANES), F32)


def _resident(shape):
    zeros = (0,) * len(shape)
    return pl.BlockSpec(shape, lambda *_: zeros, pipeline_mode=pl.Buffered(1))


def _ffn1_inproj(x, g1, wg, wu, wd, gm, win, *, seq, name):
    rows = x.shape[0]
    transposed_heads = seq is not None
    tm = ROW_TILE if transposed_heads else rows
    assert rows % tm == 0
    in_specs = [
        pl.BlockSpec((tm, D_MODEL), lambda i: (i, 0)),
        _resident((1, D_MODEL)),
        _resident(wg.shape), _resident(wu.shape), _resident(wd.shape),
        _resident((1, D_MODEL)),
        _resident(win.shape),
        _resident((QK_COLS, LANES)),
    ]
    sel = (np.arange(QK_COLS)[:, None] // HEAD_DQK == np.arange(LANES)[None, :])
    row_spec = pl.BlockSpec((tm, D_MODEL), lambda i: (i, 0))
    if transposed_heads:
        assert seq % tm == 0
        tpb = seq // tm
        batch = rows // seq
        head_t = pl.BlockSpec((1, N_HEADS, LANES, tm), lambda i: (i // tpb, 0, 0, i % tpb))
        head_n = pl.BlockSpec((1, N_HEADS, tm, LANES), lambda i: (i // tpb, 0, i % tpb, 0))
        out_specs = [row_spec, head_t, head_n, head_t,
                     pl.BlockSpec((tm, POOL_WIDTH), lambda i: (i, 0)),
                     pl.BlockSpec((1, F32_SUBLANES, LANES), lambda i: (i, 0, 0))]
        out_shape = [
            jax.ShapeDtypeStruct((rows, D_MODEL), F32),
            jax.ShapeDtypeStruct((batch, N_HEADS, LANES, seq), BF16),
            jax.ShapeDtypeStruct((batch, N_HEADS, seq, LANES), BF16),
            jax.ShapeDtypeStruct((batch, N_HEADS, LANES, seq), BF16),
            jax.ShapeDtypeStruct((rows, POOL_WIDTH), F32),
            jax.ShapeDtypeStruct((rows // tm, F32_SUBLANES, LANES), F32),
        ]
    else:
        out_specs = [row_spec, pl.BlockSpec((tm, IN_COLS), lambda i: (i, 0))]
        out_shape = [jax.ShapeDtypeStruct((rows, D_MODEL), F32),
                     jax.ShapeDtypeStruct((rows, IN_COLS), F32)]
    return pl.pallas_call(
        functools.partial(_ffn1_inproj_kernel, transposed_heads=transposed_heads),
        grid=(rows // tm,),
        in_specs=in_specs,
        out_specs=out_specs,
        out_shape=out_shape,
        scratch_shapes=[pltpu.VMEM((tm, D_MODEL), BF16), pltpu.VMEM((tm, D_MODEL), F32)],
        compiler_params=pltpu.CompilerParams(
            dimension_semantics=("parallel",), vmem_limit_bytes=VMEM_LIMIT_BYTES),
        name=name,
    )(x, g1, wg, wu, wd, gm, win, jnp.asarray(sel, BF16))


def _bias_tiles_kernel(table_ref, bdiag_ref, bmeta_ref, diag_ref, meta_ref):
    h = pl.program_id(0)
    t = ATTN_TILE

    def lookup(bucket):
        out = jnp.zeros(bucket.shape, F32)
        for b in range(N_BUCKETS):
            out = jnp.where(bucket == b, table_ref[b, h], out)
        return out

    diag_ref[0, 0] = jnp.full((t, 2 * t), table_ref[N_BUCKETS // 2 - 1, h], F32)
    diag_ref[0, N_BIAS_TILES - 1] = jnp.full((t, 2 * t), table_ref[N_BUCKETS - 1, h], F32)
    for d in range(3):
        tile = lookup(bdiag_ref[d])
        diag_ref[0, d + 1, :, 0:t] = tile
        diag_ref[0, d + 1, :, t:2 * t] = tile
    bm = bmeta_ref[...]
    near = jnp.where(bm < 0, MASK_BIAS, lookup(bm))
    far = jnp.where(bm < 0, MASK_BIAS, table_ref[N_BUCKETS // 2 - 1, h])
    for c in range(2):
        meta_ref[0, 0, :, c * t:(c + 1) * t] = near
        meta_ref[0, 1, :, c * t:(c + 1) * t] = far


def _bias_tiles(table):
    t = ATTN_TILE
    r = np.arange(t)[:, None]
    c = np.arange(t)[None, :]
    bdiag = np.stack([_rel_bucket_np((d - 1) * t + r - c) for d in range(3)])
    rm = np.arange(LANES)[:, None]
    bmeta = np.where(rm < N_META, _rel_bucket_np(rm - N_META - c), -1).astype(np.int32)
    return pl.pallas_call(
        _bias_tiles_kernel,
        grid=(N_HEADS,),
        in_specs=[
            pl.BlockSpec(memory_space=pltpu.SMEM),
            pl.BlockSpec((3, t, t), lambda h: (0, 0, 0)),
            pl.BlockSpec((LANES, t), lambda h: (0, 0)),
        ],
        out_specs=[
            pl.BlockSpec((1, N_BIAS_TILES, t, 2 * t), lambda h: (h, 0, 0, 0)),
            pl.BlockSpec((1, 2, LANES, 2 * t), lambda h: (h, 0, 0, 0)),
        ],
        out_shape=[
            jax.ShapeDtypeStruct((N_HEADS, N_BIAS_TILES, t, 2 * t), F32),
            jax.ShapeDtypeStruct((N_HEADS, 2, LANES, 2 * t), F32),
        ],
        name="bias_tiles",
    )(table, jnp.asarray(bdiag), jnp.asarray(bmeta))


def _attention_kernel(flag_ref, qt_ref, k_ref, vt_ref, km_ref, vtm_ref, bias_ref, bmeta_ref,
                      lam_ref, gain_ref, out_ref, rhs_ref, m_ref, l_ref, acc_ref, s_ref, *p_refs,
                      n_kv):
    t = ATTN_TILE
    qi = pl.program_id(1)

    for h in range(N_HEADS):
        qt = qt_ref[0, h]
        upper = lax.broadcasted_iota(jnp.int32, qt.shape, 0) < HEAD_DQK
        zero = jnp.zeros_like(qt)
        rhs_ref[h, :, 0:t] = jnp.where(upper, qt, zero)
        rhs_ref[h, :, t:2 * t] = jnp.where(upper, zero, qt)

    m_ref[...] = jnp.full(m_ref.shape, MASK_BIAS, F32)
    l_ref[...] = jnp.zeros_like(l_ref)
    acc_ref[...] = jnp.zeros_like(acc_ref)

    def key_block(j):
        start = pl.multiple_of(j * t, t)
        tile = jnp.clip(j - qi + 2, 0, N_BIAS_TILES - 1)
        return (lambda h: k_ref[0, h, pl.ds(start, t), :],
                lambda h: vt_ref[0, h, :, pl.ds(start, t)],
                lambda h: bias_ref[h, tile])

    def meta_block():
        return (lambda h: km_ref[h], lambda h: vtm_ref[h], lambda h: bmeta_ref[h, 0])

    def logits(h, kc, bias):
        return jnp.dot(kc, rhs_ref[h], preferred_element_type=F32) + bias

    def bounded_blocks():
        def weigh(h, slot, half, kc, bias):
            p = jnp.exp2(logits(h, kc, bias))
            p_refs[h][slot, half * t:half * t + kc.shape[0]] = p.astype(BF16)
            l_ref[h] += jnp.sum(p, axis=0, keepdims=True)

        def accumulate(h, slot, vtc):
            acc_ref[h] += jnp.dot(vtc, p_refs[h][slot, 0:vtc.shape[1]],
                                  preferred_element_type=F32)

        def pair_values(i):
            start = pl.multiple_of(i * 2 * t, 2 * t)
            return lambda h: vt_ref[0, h, :, pl.ds(start, 2 * t)]

        def step(slot, i, prev):
            first, second = key_block(2 * i), key_block(2 * i + 1)
            for h in range(N_HEADS):
                weigh(h, slot, 0, first[0](h), first[2](h))
                if prev is not None:
                    accumulate(h, 1 - slot, pair_values(prev)(h))
                weigh(h, slot, 1, second[0](h), second[2](h))

        n_pairs = n_kv // 2
        step(0, 0, None)

        def body(i, carry):
            j = 2 * i + 1
            step(1, j, j - 1)
            step(0, j + 1, j)
            return carry

        lax.fori_loop(0, n_pairs // 2 - 1, body, 0, unroll=KEY_PAIR_UNROLL)
        step(1, n_pairs - 1, n_pairs - 2)
        get_k, get_vt, get_bias = meta_block()
        for h in range(N_HEADS):
            weigh(h, 0, 0, get_k(h), get_bias(h))
            accumulate(h, 1, pair_values(n_pairs - 1)(h))
        for h in range(N_HEADS):
            accumulate(h, 0, get_vt(h))

    def general_blocks():
        def step(get_k, get_vt, get_bias):
            alphas = []
            for h in range(N_HEADS):
                kc = get_k(h)
                rows = kc.shape[0]
                s = logits(h, kc, get_bias(h))
                s_ref[h, 0:rows] = s
                m_old = m_ref[h]
                m_new = jnp.maximum(m_old, jnp.max(s, axis=0, keepdims=True))
                alphas.append(jnp.exp2(m_old - m_new))
                p = jnp.exp2(s_ref[h, 0:rows] - m_new)
                p_refs[h][0, 0:rows] = p.astype(BF16)
                l_ref[h] = alphas[h] * l_ref[h] + jnp.sum(p, axis=0, keepdims=True)
                m_ref[h] = m_new
            for h in range(N_HEADS):
                vtc = get_vt(h)
                acc_ref[h] = alphas[h] * acc_ref[h] + jnp.dot(
                    vtc, p_refs[h][0, 0:vtc.shape[1]], preferred_element_type=F32)

        step(*meta_block())

        def body(j, carry):
            step(*key_block(j))
            return carry

        lax.fori_loop(0, n_kv, body, 0)

    bounded = flag_ref[pl.program_id(0), qi] != 0
    pl.when(bounded)(bounded_blocks)
    pl.when(jnp.logical_not(bounded))(general_blocks)

    lp = lam_ref[...]
    lam = (jnp.exp(jnp.sum(lp[0:1] * lp[1:2], axis=-1, keepdims=True))
           - jnp.exp(jnp.sum(lp[2:3] * lp[3:4], axis=-1, keepdims=True)) + LAMBDA_INIT)
    for h in range(N_HEADS):
        l = l_ref[h]
        acc = acc_ref[h]
        o = acc[:, 0:t] / l[:, 0:t] - lam * (acc[:, t:2 * t] / l[:, t:2 * t])
        inv = lax.rsqrt(jnp.mean(o * o, axis=0, keepdims=True) + EPS)
        y = o * inv * gain_ref[...] * (1.0 - LAMBDA_INIT)
        out_ref[0, :, h * LANES:(h + 1) * LANES] = y.T.astype(BF16)


def _attention(bounded, qt, k, vt, k_meta, vt_meta, bias, bmeta, lam_params, gain_col, *, name):
    batch, _, _, seq = qt.shape
    t = ATTN_TILE
    assert seq % (4 * t) == 0
    n_kv = seq // t
    return pl.pallas_call(
        functools.partial(_attention_kernel, n_kv=n_kv),
        grid=(batch, n_kv),
        in_specs=[
            pl.BlockSpec(memory_space=pltpu.SMEM),
            pl.BlockSpec((1, N_HEADS, LANES, t), lambda b, q: (b, 0, 0, q)),
            pl.BlockSpec((1, N_HEADS, seq, LANES), lambda b, q: (b, 0, 0, 0)),
            pl.BlockSpec((1, N_HEADS, LANES, seq), lambda b, q: (b, 0, 0, 0)),
            _resident(k_meta.shape),
            _resident(vt_meta.shape),
            _resident(bias.shape),
            pl.BlockSpec((N_HEADS, 1, LANES, 2 * t), lambda b, q: (0, jnp.minimum(q, 1), 0, 0)),
            _resident((4, HEAD_DQK)),
            _resident((HEAD_DV, 1)),
        ],
        out_specs=pl.BlockSpec((1, t, ATTN_WIDTH), lambda b, q: (b, q, 0)),
        out_shape=jax.ShapeDtypeStruct((batch, seq, ATTN_WIDTH), BF16),
        scratch_shapes=[
            pltpu.VMEM((N_HEADS, LANES, 2 * t), BF16),
            pltpu.VMEM((N_HEADS, 1, 2 * t), F32),
            pltpu.VMEM((N_HEADS, 1, 2 * t), F32),
            pltpu.VMEM((N_HEADS, HEAD_DV, 2 * t), F32),
            pltpu.VMEM((N_HEADS, t, 2 * t), F32),
        ] + [pltpu.VMEM((2, 2 * t, 2 * t), BF16) for _ in range(N_HEADS)],
        compiler_params=pltpu.CompilerParams(
            dimension_semantics=("parallel", "arbitrary"),
            vmem_limit_bytes=VMEM_LIMIT_BYTES,
        ),
        name=name,
    )(bounded, qt, k, vt, k_meta, vt_meta, bias, bmeta, lam_params, gain_col)


def _bounded_flags(stats, k_meta_sq, v_meta_max, table, *, batch, seq):
    tiles = seq // ROW_TILE
    groups = 2 * N_HEADS
    q_sq = stats[:, 0, :groups].reshape(batch, tiles, groups)
    k_sq = stats[:, 1, :groups].reshape(batch, tiles, groups).max(axis=1)
    k_sq = jnp.maximum(k_sq, k_meta_sq[None, :])
    v_max = jnp.maximum(stats[:, 2, 0].reshape(batch, tiles).max(axis=1), v_meta_max)
    logit_bound = (jnp.sqrt(q_sq * k_sq[:, None, :]).max(axis=-1) * NORM_MARGIN
                   + jnp.max(jnp.abs(table)))
    ok = jnp.logical_and(logit_bound < BOUNDED_LOGIT_LIMIT,
                         (v_max < BOUNDED_VALUE_LIMIT)[:, None])
    return jnp.repeat(ok.astype(jnp.int32), ROW_TILE // ATTN_TILE, axis=1)


def _mix_ffn2_kernel(h1_ref, attn_ref, pool_ref, prev_ref, next_ref, mhalo_ref, pw_ref, ps_ref,
                     wout_ref, g2_ref, wg_ref, wu_ref, wd_ref, gf_ref, y_ref,
                     x_ref, sum2_ref, sum4_ref, sum8_ref, sum16_ref, pooled_ref, xn_ref, acc_ref,
                     *, tiles_per_seq):
    sum_refs = (sum2_ref, sum4_ref, sum8_ref, sum16_ref)
    assert tuple(2 << k for k in range(len(sum_refs))) == POOL_WINDOWS
    tm = h1_ref.shape[0]
    halo = POOL_HALO
    lead = 2 * halo
    ext = tm + 2 * halo
    tile = pl.program_id(0) % tiles_per_seq
    last_tile = tile == tiles_per_seq - 1
    x_ref[0:halo, :] = jnp.zeros((halo, POOL_WIDTH), F32)
    x_ref[halo:lead, :] = jnp.where(tile == 0, mhalo_ref[...], prev_ref[...])
    x_ref[lead:lead + tm, :] = pool_ref[...]
    x_ref[lead + tm:, :] = jnp.where(last_tile, 0.0, next_ref[...])

    src = x_ref
    for k, dst in enumerate(sum_refs):
        cols = slice(k * POOL_GROUP, POOL_WIDTH)
        reach = 1 << k
        dst[0:halo, :] = jnp.zeros((halo, POOL_WIDTH), F32)
        dst[halo:halo + ext, cols] = (src[halo:halo + ext, cols]
                                      + src[halo - reach:halo - reach + ext, cols])
        src = dst

    def pooled_rows(g, w, rows, cnt):
        cols = slice(g * POOL_GROUP, (g + 1) * POOL_GROUP)
        end = w // 2 - 1
        total = sum_refs[g][lead + rows.start + end:lead + rows.stop + end, cols]
        return (total / cnt - x_ref[lead + rows.start:lead + rows.stop, cols]).astype(BF16)

    for g, w in enumerate(POOL_WINDOWS):
        pooled_ref[:, g * POOL_GROUP:(g + 1) * POOL_GROUP] = pooled_rows(
            g, w, slice(0, tm), float(w))

    @pl.when(last_tile)
    def _():
        rows = slice(tm - halo, tm)
        remaining = halo - lax.broadcasted_iota(jnp.int32, (halo, 1), 0)
        for g, w in enumerate(POOL_WINDOWS):
            cnt = jnp.minimum(w, remaining + w // 2).astype(F32)
            pooled_ref[rows, g * POOL_GROUP:(g + 1) * POOL_GROUP] = pooled_rows(g, w, rows, cnt)

    xn_ref[:, 0:ATTN_WIDTH] = attn_ref[...]
    for g in range(len(POOL_WINDOWS)):
        cols = slice(g * POOL_GROUP, (g + 1) * POOL_GROUP)
        mixed = jnp.dot(pooled_ref[:, cols], pw_ref[g], preferred_element_type=F32) * ps_ref[:, cols]
        xn_ref[:, ATTN_WIDTH + g * POOL_GROUP:ATTN_WIDTH + (g + 1) * POOL_GROUP] = (
            mixed.astype(BF16))
    h2 = h1_ref[...] + jnp.dot(xn_ref[...], wout_ref[...], preferred_element_type=F32)
    xn_ref[...] = _rms(h2, g2_ref[...]).astype(BF16)
    _swiglu_into(acc_ref, xn_ref, wg_ref, wu_ref, wd_ref)
    y_ref[...] = _rms(h2 + 0.5 * acc_ref[...], gf_ref[...])


def _mix_ffn2(h1, attn, pool, meta_halo, pw, ps, wout, g2, wg, wu, wd, gf, *, seq, name):
    rows = h1.shape[0]
    tm = ROW_TILE
    halo = POOL_HALO
    assert rows % tm == 0 and seq % tm == 0 and tm % halo == 0
    halo_blocks = rows // halo
    per_tile = tm // halo
    sums = pltpu.VMEM((tm + 3 * halo, POOL_WIDTH), F32)
    return pl.pallas_call(
        functools.partial(_mix_ffn2_kernel, tiles_per_seq=seq // tm),
        grid=(rows // tm,),
        in_specs=[
            pl.BlockSpec((tm, D_MODEL), lambda i: (i, 0)),
            pl.BlockSpec((tm, ATTN_WIDTH), lambda i: (i, 0)),
            pl.BlockSpec((tm, POOL_WIDTH), lambda i: (i, 0)),
            pl.BlockSpec((halo, POOL_WIDTH), lambda i: (jnp.maximum(i * per_tile - 1, 0), 0)),
            pl.BlockSpec((halo, POOL_WIDTH),
                         lambda i: (jnp.minimum((i + 1) * per_tile, halo_blocks - 1), 0)),
            _resident((halo, POOL_WIDTH)),
            _resident(pw.shape),
            _resident((1, POOL_WIDTH)),
            _resident(wout.shape),
            _resident((1, D_MODEL)),
            _resident(wg.shape), _resident(wu.shape), _resident(wd.shape),
            _resident((1, D_MODEL)),
        ],
        out_specs=pl.BlockSpec((tm, D_MODEL), lambda i: (i, 0)),
        out_shape=jax.ShapeDtypeStruct((rows, D_MODEL), F32),
        scratch_shapes=[
            sums, sums, sums, sums, sums,
            pltpu.VMEM((tm, POOL_WIDTH), BF16),
            pltpu.VMEM((tm, D_MODEL), BF16),
            pltpu.VMEM((tm, D_MODEL), F32),
        ],
        compiler_params=pltpu.CompilerParams(
            dimension_semantics=("parallel",), vmem_limit_bytes=VMEM_LIMIT_BYTES),
        name=name,
    )(h1, attn, pool, pool, pool, meta_halo, pw, ps, wout, g2, wg, wu, wd, gf)


def _chunk_rows(w):
    return w.reshape(N_FF_CHUNKS, FF_CHUNK, D_MODEL).astype(BF16)


def kernel(x_prompt, x_sample, meta_tokens, rel_bias_table, norm_ffn1, ffn1_w_gate, ffn1_w_up,
           ffn1_w_down, norm_mix, w_in, lambda_q1, lambda_k1, lambda_q2, lambda_k2, subln_gain,
           pool_w, pool_scale, w_out, norm_ffn2, ffn2_w_gate, ffn2_w_up, ffn2_w_down, norm_final):
    layer = 0
    g1 = norm_ffn1[layer].reshape(1, D_MODEL)
    gm = norm_mix[layer].reshape(1, D_MODEL)
    g2 = norm_ffn2[layer].reshape(1, D_MODEL)
    gf = norm_final.reshape(1, D_MODEL)
    f1 = (ffn1_w_gate[layer].astype(BF16), ffn1_w_up[layer].astype(BF16),
          _chunk_rows(ffn1_w_down[layer]))
    f2 = (ffn2_w_gate[layer].astype(BF16), ffn2_w_up[layer].astype(BF16),
          _chunk_rows(ffn2_w_down[layer]))
    win = w_in[layer].astype(BF16)
    wout = w_out[layer].astype(BF16)
    pw = pool_w[layer].astype(BF16)
    ps = pool_scale[layer].reshape(1, POOL_WIDTH)
    lam_params = jnp.stack([lambda_q1[layer], lambda_k1[layer], lambda_q2[layer],
                            lambda_k2[layer]]).astype(F32)
    gain_col = subln_gain[layer].reshape(HEAD_DV, 1)
    table = rel_bias_table.astype(F32) * LOG2E

    _, u_meta = _ffn1_inproj(meta_tokens.astype(F32), g1, *f1, gm, win, seq=None,
                             name="ffn1_inproj_meta")
    pad = LANES - N_META
    k_meta = u_meta[:, QK_COLS:2 * QK_COLS].reshape(N_META, N_HEADS, LANES)
    k_meta = jnp.pad(k_meta.transpose(1, 0, 2), ((0, 0), (0, pad), (0, 0))).astype(BF16)
    v_meta = u_meta[:, 2 * QK_COLS:2 * QK_COLS + ATTN_WIDTH].reshape(N_META, N_HEADS, HEAD_DV)
    vt_meta = jnp.pad(v_meta.transpose(1, 2, 0), ((0, 0), (0, 0), (0, pad))).astype(BF16)
    meta_halo = u_meta[N_META - POOL_HALO:, 2 * QK_COLS + ATTN_WIDTH:]
    k_meta_sq = jnp.square(u_meta[:, QK_COLS:2 * QK_COLS]).reshape(
        N_META, 2 * N_HEADS, HEAD_DQK).sum(axis=-1).max(axis=0)
    v_meta_max = jnp.max(jnp.abs(v_meta))

    bias, bmeta = _bias_tiles(table)

    def encode(x, tag):
        batch, seq, _ = x.shape
        rows = batch * seq
        h1, qt, k, vt, pool, stats = _ffn1_inproj(x.reshape(rows, D_MODEL), g1, *f1, gm, win,
                                                  seq=seq, name="ffn1_inproj_" + tag)
        bounded = _bounded_flags(stats, k_meta_sq, v_meta_max, table, batch=batch, seq=seq)
        attn = _attention(bounded, qt, k, vt, k_meta, vt_meta, bias, bmeta, lam_params, gain_col,
                          name="attention_" + tag)
        y = _mix_ffn2(h1, attn.reshape(rows, ATTN_WIDTH), pool, meta_halo, pw, ps, wout, g2, *f2,
                      gf, seq=seq, name="mix_ffn2_" + tag)
        return y.reshape(batch, seq, D_MODEL)

    return (encode(x_prompt, "prompt"), encode(x_sample, "sample"))
```
